```python
import math
import jax, jax.numpy as jnp
from jax import lax
import numpy as np

D_MODEL = 1024
BATCH = 1
SEQ = 16384
DEPTH = 2
DEC_BATCH = 8
DEC_SEQ = 32
PAST_LEN = 2048

CHUNK = 64
CONV_W = 3
D_CONV = 1024
SSM_GROUP = 16
SSM_STATE = 64
D_SSM = 1024
SSM_GROUPS = D_SSM // SSM_GROUP
XA_HEADS = 4
XA_HEAD_DIM = 256
D_XA = XA_HEADS * XA_HEAD_DIM
N_MEM = 256
D_MIX = D_CONV + D_SSM + D_XA
D_IN = 4 * D_CONV + 2 * D_SSM + 2 * D_XA
SPLITS = (D_CONV, 2 * D_CONV, 3 * D_CONV, 4 * D_CONV,
          4 * D_CONV + D_SSM, 4 * D_CONV + 2 * D_SSM, 4 * D_CONV + 2 * D_SSM + D_XA)
EPS = 1e-6

kernel_name = "hybrid_conv_s5_memxattn_stream_step"


def rmsnorm(x, g):
    xf = x.astype(jnp.float32)
    y = xf * lax.rsqrt(jnp.mean(xf * xf, axis=-1, keepdims=True) + EPS)
    return (y * g.astype(jnp.float32)).astype(x.dtype)


def mem_kv(mem, g, w_kv):
    b, n, _ = mem.shape
    k, v = jnp.split(rmsnorm(mem, g) @ w_kv, 2, axis=-1)
    return (k.reshape(b, n, XA_HEADS, XA_HEAD_DIM), v.reshape(b, n, XA_HEADS, XA_HEAD_DIM))


def short_conv(u, buf, w, bias):
    t = u.shape[1]
    full = jnp.concatenate([buf.astype(u.dtype), u], axis=1)
    y = sum((full[:, j:j + t] * w[j] for j in range(CONV_W)), bias)
    return y, full[:, -(CONV_W - 1):]


def ssm_discretize(lam_re, lam_im, log_dt, b_re, b_im, c_re, c_im):
    f32 = jnp.float32
    lam = lax.complex(lam_re.astype(f32), lam_im.astype(f32))
    dt = jnp.exp(log_dt.astype(f32))[:, None]
    lam_bar = jnp.exp(lam * dt)
    b = lax.complex(b_re.astype(f32), b_im.astype(f32))
    b_bar = ((lam_bar - 1.0) / lam)[..., None] * b
    c = lax.complex(c_re.astype(f32), c_im.astype(f32))
    return lam_bar, b_bar, c


def _lin_comb(left, right):
    a_l, b_l = left
    a_r, b_r = right
    return a_l * a_r, a_r * b_l + b_r


def ssm_block(h0, u, lam_bar, b_bar, c):
    bu = jnp.einsum('gni,blgi->blgn', b_bar, u.astype(jnp.complex64))
    bu = bu.at[:, 0].add(lam_bar * h0)
    a = jnp.broadcast_to(lam_bar, bu.shape)
    _, h = lax.associative_scan(_lin_comb, (a, bu), axis=1)
    y = jnp.einsum('gin,blgn->blgi', c, h).real
    return h[:, -1], y


def ssm_scan(u, h0, lam_bar, b_bar, c):
    b, t, g, i = u.shape
    if t <= CHUNK:
        return ssm_block(h0, u, lam_bar, b_bar, c)
    nb = t // CHUNK
    ub = u.reshape(b, nb, CHUNK, g, i).transpose(1, 0, 2, 3, 4)
    h_last, yb = lax.scan(lambda h, ublk: ssm_block(h, ublk, lam_bar, b_bar, c), h0, ub)
    return h_last, yb.transpose(1, 0, 2, 3, 4).reshape(b, t, g, i)


def layer(x, conv_buf, h0, k, v, norm_g, w_in, conv_w, conv_b, lam_re, lam_im, log_dt,
          b_re, b_im, c_re, c_im, ssm_d, glu_w, glu_b, w_out):
    f32 = jnp.float32
    bsz, t, _ = x.shape
    h = rmsnorm(x, norm_g)
    z = h @ w_in
    cx, cb, cc, cg, su, sg, q, qg = jnp.split(z, SPLITS, axis=-1)
    conv_out, new_buf = short_conv(cc * cx, conv_buf, conv_w, conv_b)
    y_conv = cb * conv_out * jax.nn.silu(cg)
    lam_bar, b_bar, c = ssm_discretize(lam_re, lam_im, log_dt, b_re, b_im, c_re, c_im)
    uf = su.astype(f32)
    h_last, ys = ssm_scan(uf.reshape(bsz, t, SSM_GROUPS, SSM_GROUP), h0, lam_bar, b_bar, c)
    ys = jax.nn.gelu(ys.reshape(bsz, t, D_SSM) + ssm_d.astype(f32) * uf).astype(x.dtype)
    y_ssm = ys * jax.nn.sigmoid(ys @ glu_w + glu_b) * jax.nn.silu(sg)
    qh = q.reshape(bsz, t, XA_HEADS, XA_HEAD_DIM).astype(f32)
    s = jnp.einsum('bthd,bmhd->bhtm', qh, k.astype(f32)) * (XA_HEAD_DIM ** -0.5)
    p = jax.nn.softmax(s, axis=-1)
    o = jnp.einsum('bhtm,bmhd->bthd', p, v.astype(f32)).reshape(bsz, t, D_XA).astype(x.dtype)
    y_xa = o * jax.nn.silu(qg)
    out = jnp.concatenate([y_conv, y_ssm, y_xa], axis=-1) @ w_out
    return x + out, new_buf, h_last


def setup_inputs(seed: int = 0) -> dict:
    key = jax.random.key(seed)
    ks = jax.random.split(key, 32)
    nrm = lambda k, s: jax.random.normal(k, s, jnp.float32)
    lam_im = jnp.broadcast_to(jnp.pi * jnp.arange(SSM_STATE, dtype=jnp.float32), (DEPTH, SSM_GROUPS, SSM_STATE))
    return {
        "x_prompt": nrm(ks[0], (BATCH, SEQ, D_MODEL)),
        "x_sample": nrm(ks[1], (DEC_BATCH, DEC_SEQ, D_MODEL)),
        "mem_prompt": nrm(ks[2], (BATCH, N_MEM, D_MODEL)),
        "cache_conv": nrm(ks[3], (DEPTH, DEC_BATCH, CONV_W - 1, D_CONV)),
        "state_ssm_re": 0.3 * nrm(ks[4], (DEPTH, DEC_BATCH, SSM_GROUPS, SSM_STATE)),
        "state_ssm_im": 0.3 * nrm(ks[5], (DEPTH, DEC_BATCH, SSM_GROUPS, SSM_STATE)),
        "cache_mem_k": nrm(ks[6], (DEPTH, DEC_BATCH, N_MEM, XA_HEADS, XA_HEAD_DIM)),
        "cache_mem_v": nrm(ks[7], (DEPTH, DEC_BATCH, N_MEM, XA_HEADS, XA_HEAD_DIM)),
        "norm_g": 1.0 + 0.01 * nrm(ks[8], (DEPTH, D_MODEL)),
        "w_in": nrm(ks[9], (DEPTH, D_MODEL, D_IN)) * D_MODEL ** -0.5,
        "conv_w": nrm(ks[10], (DEPTH, CONV_W, D_CONV)) * CONV_W ** -0.5,
        "conv_b": 0.01 * nrm(ks[11], (DEPTH, D_CONV)),
        "ssm_lambda_re": -0.5 + 0.01 * nrm(ks[12], (DEPTH, SSM_GROUPS, SSM_STATE)),
        "ssm_lambda_im": lam_im + 0.01 * nrm(ks[13], (DEPTH, SSM_GROUPS, SSM_STATE)),
        "ssm_log_dt": jax.random.uniform(ks[14], (DEPTH, SSM_GROUPS), jnp.float32,
                                         math.log(1e-3), math.log(1e-1)),
        "ssm_b_re": nrm(ks[15], (DEPTH, SSM_GROUPS, SSM_STATE, SSM_GROUP)) * (0.5 / SSM_GROUP) ** 0.5,
        "ssm_b_im": nrm(ks[16], (DEPTH, SSM_GROUPS, SSM_STATE, SSM_GROUP)) * (0.5 / SSM_GROUP) ** 0.5,
        "ssm_c_re": 0.5 * nrm(ks[17], (DEPTH, SSM_GROUPS, SSM_GROUP, SSM_STATE)),
        "ssm_c_im": 0.5 * nrm(ks[18], (DEPTH, SSM_GROUPS, SSM_GROUP, SSM_STATE)),
        "ssm_d": nrm(ks[19], (DEPTH, D_SSM)),
        "ssm_glu_w": nrm(ks[20], (DEPTH, D_SSM, D_SSM)) * D_SSM ** -0.5,
        "ssm_glu_b": 0.01 * nrm(ks[21], (DEPTH, D_SSM)),
        "mem_norm_g": 1.0 + 0.01 * nrm(ks[22], (DEPTH, D_MODEL)),
        "w_kv": nrm(ks[23], (DEPTH, D_MODEL, 2 * D_XA)) * D_MODEL ** -0.5,
        "w_out": nrm(ks[24], (DEPTH, D_MIX, D_MODEL)) * D_MIX ** -0.5,
        "final_norm_g": 1.0 + 0.01 * nrm(ks[25], (D_MODEL,)),
    }


def reference(x_prompt, x_sample, mem_prompt, cache_conv, state_ssm_re, state_ssm_im, cache_mem_k,
              cache_mem_v, norm_g, w_in, conv_w, conv_b, ssm_lambda_re, ssm_lambda_im, ssm_log_dt,
              ssm_b_re, ssm_b_im, ssm_c_re, ssm_c_im, ssm_d, ssm_glu_w, ssm_glu_b, mem_norm_g, w_kv,
              w_out, final_norm_g):
    assert x_sample.shape[1] <= CHUNK
    f32 = jnp.float32
    bp = x_prompt.shape[0]
    xp, xs = x_prompt, x_sample
    p_conv, p_re, p_im, p_k, p_v = [], [], [], [], []
    s_conv, s_re, s_im = [], [], []
    for l in range(DEPTH):
        lw = (norm_g[l], w_in[l], conv_w[l], conv_b[l], ssm_lambda_re[l], ssm_lambda_im[l],
              ssm_log_dt[l], ssm_b_re[l], ssm_b_im[l], ssm_c_re[l], ssm_c_im[l], ssm_d[l],
              ssm_glu_w[l], ssm_glu_b[l], w_out[l])
        kp, vp = mem_kv(mem_prompt, mem_norm_g[l], w_kv[l])
        conv0 = jnp.zeros((bp, CONV_W - 1, D_CONV), xp.dtype)
        h0p = jnp.zeros((bp, SSM_GROUPS, SSM_STATE), jnp.complex64)
        xp, bufp, hp = layer(xp, conv0, h0p, kp, vp, *lw)
        p_conv.append(bufp)
        p_re.append(hp.real.astype(x_prompt.dtype))
        p_im.append(hp.imag.astype(x_prompt.dtype))
        p_k.append(kp)
        p_v.append(vp)
        h0s = lax.complex(state_ssm_re[l].astype(f32), state_ssm_im[l].astype(f32))
        xs, bufs, hs = layer(xs, cache_conv[l], h0s, cache_mem_k[l], cache_mem_v[l], *lw)
        s_conv.append(bufs)
        s_re.append(hs.real.astype(x_sample.dtype))
        s_im.append(hs.imag.astype(x_sample.dtype))
    y_prompt = rmsnorm(xp, final_norm_g)
    y_sample = rmsnorm(xs, final_norm_g)
    return (y_prompt, y_sample, jnp.stack(p_conv), jnp.stack(p_re), jnp.stack(p_im), jnp.stack(p_k),
            jnp.stack(p_v), jnp.stack(s_conv), jnp.stack(s_re), jnp.stack(s_im))
```

```python
import functools

import jax
import jax.numpy as jnp
from jax import lax
from jax.experimental import pallas as pl
from jax.experimental.pallas import tpu as pltpu

F32 = jnp.float32
BF16 = jnp.bfloat16

D = 1024
N_SEG = 8
SSM_GROUP = 16
SSM_STATE = 64
SSM_GROUPS = 64
N_SLAB = 8
SLAB = 128
SLAB_STATES = 512
HEADS = 4
HEAD_DIM = 256
N_MEM = 256
CONV_CHUNK = 512
EPS = 1e-6
MASK_VALUE = -1e30
VMEM_LIMIT = 60000 * 1024


def _dot(a, b):
    return jnp.dot(a, b, preferred_element_type=F32)


def _rms(x, g):
    ms = jnp.mean(x * x, axis=-1, keepdims=True)
    return x * lax.rsqrt(ms + EPS) * g


def _disc_kernel(lre_ref, lim_ref, ldt_ref, bre_ref, bim_ref,
                 pre_ref, pim_ref, bbre_ref, bbim_ref, *, ls):
    lr = lre_ref[...]
    li = lim_ref[...]
    dt = jnp.exp(ldt_ref[...])
    ea = jnp.exp(lr * dt)
    ar = ea * jnp.cos(li * dt)
    ai = ea * jnp.sin(li * dt)
    nr = ar - 1.0
    den = lr * lr + li * li
    fr = (nr * lr + ai * li) / den
    fi = (ai * lr - nr * li) / den
    for i in range(SSM_GROUP):
        br = bre_ref[i]
        bi = bim_ref[i]
        bbre_ref[i] = fr * br - fi * bi
        bbim_ref[i] = fr * bi + fi * br
    pr, pi = ar, ai
    for k in range(ls):
        pre_ref[k] = pr
        pim_ref[k] = pi
        if k + 1 < ls:
            pr, pi = pr * ar - pi * ai, pr * ai + pi * ar
    sr, si = pr, pi
    cr, ci = sr, si
    for m in range(N_SEG):
        pre_ref[ls + m] = cr
        pim_ref[ls + m] = ci
        cr, ci = cr * sr - ci * si, cr * si + ci * sr


def _discretize(lam_re, lam_im, log_dt, b_re, b_im, ls):
    g, n = lam_re.shape
    bt_re = jnp.transpose(b_re, (2, 0, 1))
    bt_im = jnp.transpose(b_im, (2, 0, 1))
    npow = ls + N_SEG
    return pl.pallas_call(
        functools.partial(_disc_kernel, ls=ls),
        out_shape=(jax.ShapeDtypeStruct((npow, g, n), F32),
                   jax.ShapeDtypeStruct((npow, g, n), F32),
                   jax.ShapeDtypeStruct((SSM_GROUP, g, n), F32),
                   jax.ShapeDtypeStruct((SSM_GROUP, g, n), F32)),
        name="ssm_discretize",
    )(lam_re, lam_im, log_dt.reshape(g, 1), bt_re, bt_im)


def _slab_cols(a):
    return a.reshape(a.shape[:-2] + (N_SLAB, SLAB_STATES))


def _ssm_tables(p_re, p_im, bb_re, bb_im, c_re, c_im, ls):
    eye = jnp.eye(N_SEG, dtype=F32)

    def bblock(bb):
        b4 = jnp.transpose(bb, (1, 0, 2)).reshape(N_SLAB, 8, SSM_GROUP, SSM_STATE)
        return jnp.einsum('sgin,gh->sgihn', b4, eye).reshape(N_SLAB, SLAB, SLAB_STATES)

    def cblock(c):
        c4 = c.reshape(N_SLAB, 8, SSM_GROUP, SSM_STATE)
        return jnp.einsum('sgin,gh->sgnhi', c4, eye).reshape(N_SLAB, SLAB_STATES, SLAB)

    bblk = jnp.concatenate([bblock(bb_re), bblock(bb_im)], axis=-1).astype(BF16)
    cblk = jnp.concatenate([cblock(c_re), -cblock(c_im)], axis=1).astype(BF16)

    def reim(k):
        return jnp.concatenate([_slab_cols(p_re[k]), _slab_cols(p_im[k])], axis=-1)

    lam = jnp.broadcast_to(reim(0)[:, None, :], (N_SLAB, N_SEG, 2 * SLAB_STATES))
    pw = jnp.stack([reim(k) for k in range(ls)], axis=1)
    sub = jnp.arange(N_SEG)[None, :, None]
    q = []
    for sh in (1, 2, 4):
        v = jnp.broadcast_to(reim(ls + sh - 1)[:, None, :], (N_SLAB, N_SEG, 2 * SLAB_STATES))
        q.append(jnp.where(sub >= sh, v, 0.0))
    q.append(jnp.stack([reim(ls + m) for m in range(N_SEG)], axis=1))
    q = jnp.stack(q, axis=1)
    return bblk, cblk, lam, pw, q


def _kv_kernel(mem_ref, g_ref, w_ref, k_ref, v_ref):
    h = _rms(mem_ref[...], g_ref[...]).astype(BF16)
    kv = _dot(h, w_ref[...].astype(BF16))
    k_ref[...] = kv[:, :D]
    v_ref[...] = kv[:, D:]


def _mem_kv(mem, g, w_kv):
    n = mem.shape[0]
    return pl.pallas_call(
        _kv_kernel,
        out_shape=(jax.ShapeDtypeStruct((n, D), F32), jax.ShapeDtypeStruct((n, D), F32)),
        compiler_params=pltpu.CompilerParams(vmem_limit_bytes=VMEM_LIMIT),
        name="mem_kv",
    )(mem, g.reshape(1, D), w_kv)


def _cmul(ar, ai, br, bi):
    return ar * br - ai * bi, ar * bi + ai * br


def _layer_kernel(*refs, ls, chained, final_norm):
    rows = N_SEG * ls
    it = iter(refs)
    x_ref, ng_ref, win_ref, cw_ref, cb_ref = (next(it) for _ in range(5))
    bblk_ref, cblk_ref, lam_ref = (next(it) for _ in range(3))
    if chained:
        pw_ref, q_ref = next(it), next(it)
    dssm_ref, gw_ref, gb_ref, k_ref, v_ref, wout_ref, fg_ref = (next(it) for _ in range(7))
    if not chained:
        cinit_ref, h0_ref = next(it), next(it)
    o_ref, convout_ref, ssmout_ref = (next(it) for _ in range(3))
    xp, hbf, vext, bu, hb, su32, ys32, ysb, mix = (next(it) for _ in range(9))
    if chained:
        tail, carry = next(it), next(it)

        @pl.when(pl.program_id(0) == 0)
        def _():
            tail[...] = jnp.zeros_like(tail)
            carry[...] = jnp.zeros_like(carry)

    xp[...] = jnp.concatenate([x_ref[0, :, i * D:(i + 1) * D] for i in range(ls)], axis=0)
    hbf[...] = _rms(xp[...], ng_ref[...]).astype(BF16)

    sub_c = lax.broadcasted_iota(jnp.int32, (N_SEG, CONV_CHUNK), 0)
    for c in range(D // CONV_CHUNK):
        lo = c * CONV_CHUNK
        sl = slice(lo, lo + CONV_CHUNK)
        h = hbf[...]
        cx = _dot(h, win_ref[:, lo:lo + CONV_CHUNK])
        cc = _dot(h, win_ref[:, 2 * D + lo:2 * D + lo + CONV_CHUNK])
        v = cc * cx
        last2 = v[rows - 16:rows - 8]
        last1 = v[rows - 8:rows]
        if chained:
            b2 = pltpu.roll(jnp.where(sub_c == N_SEG - 1, tail[0:8, sl], last2), 1, 0)
            b1 = pltpu.roll(jnp.where(sub_c == N_SEG - 1, tail[8:16, sl], last1), 1, 0)
            tail[0:8, sl] = last2
            tail[8:16, sl] = last1
        else:
            b2 = cinit_ref[0, :, sl]
            b1 = cinit_ref[1, :, sl]
        convout_ref[0, :, sl] = last2
        convout_ref[1, :, sl] = last1
        vext[0:8] = b2
        vext[8:16] = b1
        vext[16:16 + rows] = v
        conv = (cw_ref[0:1, sl] * vext[0:rows] + cw_ref[1:2, sl] * vext[8:8 + rows]
                + cw_ref[2:3, sl] * vext[16:16 + rows] + cb_ref[:, sl])
        cb = _dot(h, win_ref[:, D + lo:D + lo + CONV_CHUNK])
        cg = _dot(h, win_ref[:, 3 * D + lo:3 * D + lo + CONV_CHUNK])
        mix[:, sl] = (cb * conv * jax.nn.silu(cg)).astype(BF16)

    su32[...] = _dot(hbf[...], win_ref[:, 4 * D:5 * D])
    sub_s = lax.broadcasted_iota(jnp.int32, (N_SEG, SLAB_STATES), 0)
    re = slice(0, SLAB_STATES)
    im = slice(SLAB_STATES, 2 * SLAB_STATES)
    for s in range(N_SLAB):
        cs = slice(s * SLAB, (s + 1) * SLAB)
        bu[...] = _dot(su32[:, cs].astype(BF16), bblk_ref[s])
        lr = lam_ref[s, :, re]
        li = lam_ref[s, :, im]
        if chained:
            h0r = jnp.zeros((N_SEG, SLAB_STATES), F32)
            h0i = h0r
        else:
            h0r = h0_ref[s, :, re]
            h0i = h0_ref[s, :, im]

        def scan_body(i, hc, lr=lr, li=li):
            hr, hi = hc
            r0 = pl.multiple_of(i * 8, 8)
            nr = lr * hr - li * hi + bu[pl.ds(r0, 8), re]
            ni = lr * hi + li * hr + bu[pl.ds(r0, 8), im]
            bu[pl.ds(r0, 8), re] = nr
            bu[pl.ds(r0, 8), im] = ni
            return nr, ni

        er, ei = lax.fori_loop(0, ls, scan_body, (h0r, h0i))

        if chained:
            c0r = carry[s, 7:8, re]
            c0i = carry[s, 7:8, im]
            tr, ti = er, ei
            for k, sh in enumerate((1, 2, 4)):
                dr, di = _cmul(q_ref[s, k, :, re], q_ref[s, k, :, im],
                               pltpu.roll(tr, sh, 0), pltpu.roll(ti, sh, 0))
                tr, ti = tr + dr, ti + di
            dr, di = _cmul(q_ref[s, 3, :, re], q_ref[s, 3, :, im], c0r, c0i)
            tr, ti = tr + dr, ti + di
            carry[s, :, re] = tr
            carry[s, :, im] = ti
            ssmout_ref[s, :, re] = tr
            ssmout_ref[s, :, im] = ti
            sr = jnp.where(sub_s == 0, c0r, pltpu.roll(tr, 1, 0))
            si = jnp.where(sub_s == 0, c0i, pltpu.roll(ti, 1, 0))

            def fix_body(i, _, sr=sr, si=si, s=s):
                r0 = pl.multiple_of(i * 16, 16)
                blocks = []
                for d in range(2):
                    dr, di = _cmul(pw_ref[s, pl.ds(2 * i + d, 1), re],
                                   pw_ref[s, pl.ds(2 * i + d, 1), im], sr, si)
                    blocks.append(jnp.concatenate(
                        [bu[pl.ds(r0 + 8 * d, 8), re] + dr, bu[pl.ds(r0 + 8 * d, 8), im] + di],
                        axis=1))
                hb[pl.ds(r0, 16), :] = jnp.concatenate(blocks, axis=0).astype(BF16)
                return 0

            lax.fori_loop(0, ls // 2, fix_body, 0)
        else:
            ssmout_ref[s, :, re] = er
            ssmout_ref[s, :, im] = ei
            hb[...] = bu[...].astype(BF16)

        y = _dot(hb[...], cblk_ref[s]) + dssm_ref[:, cs] * su32[:, cs]
        ys = jax.nn.gelu(y)
        ys32[:, cs] = ys
        ysb[:, cs] = ys.astype(BF16)

    gate = _dot(ysb[...], gw_ref[...]) + gb_ref[...]
    sg = _dot(hbf[...], win_ref[:, 5 * D:6 * D])
    mix[:, D:2 * D] = (ys32[...] * jax.nn.sigmoid(gate) * jax.nn.silu(sg)).astype(BF16)

    n_keys = k_ref.shape[0]
    if not chained:
        row_seq = lax.broadcasted_iota(jnp.int32, (rows, n_keys), 0) % N_SEG
        key_seq = lax.broadcasted_iota(jnp.int32, (rows, n_keys), 1) // N_MEM
        own = row_seq == key_seq
    scale = HEAD_DIM ** -0.5
    for hd in range(HEADS):
        hs = slice(hd * HEAD_DIM, (hd + 1) * HEAD_DIM)
        q = _dot(hbf[...], win_ref[:, 6 * D + hd * HEAD_DIM:6 * D + (hd + 1) * HEAD_DIM])
        sc = lax.dot_general(q.astype(BF16), k_ref[:, hs], (((1,), (1,)), ((), ())),
                             preferred_element_type=F32) * scale
        if not chained:
            sc = jnp.where(own, sc, MASK_VALUE)
        e = jnp.exp(sc - jnp.max(sc, axis=-1, keepdims=True))
        p = e / jnp.sum(e, axis=-1, keepdims=True)
        o = _dot(p.astype(BF16), v_ref[:, hs])
        qg = _dot(hbf[...], win_ref[:, 7 * D + hd * HEAD_DIM:7 * D + (hd + 1) * HEAD_DIM])
        mix[:, 2 * D + hd * HEAD_DIM:2 * D + (hd + 1) * HEAD_DIM] = (o * jax.nn.silu(qg)).astype(BF16)

    y = xp[...] + _dot(mix[...], wout_ref[...])
    if final_norm:
        y = _rms(y, fg_ref[...])
    for i in range(ls):
        o_ref[0, :, i * D:(i + 1) * D] = y[8 * i:8 * i + 8]


def _vmem_spec():
    return pl.BlockSpec(memory_space=pltpu.VMEM)


def _layer(x3, lw, tables, k, v, final_g, *, ls, chained, final_norm, cinit=None, h0=None, name):
    nt = x3.shape[0]
    rows = N_SEG * ls
    bblk, cblk, lam, pw, q = tables
    tile = pl.BlockSpec((1, N_SEG, ls * D), lambda t: (t, 0, 0))
    args = [x3, lw['norm_g'], lw['w_in'], lw['conv_w'], lw['conv_b'], bblk, cblk, lam]
    if chained:
        args += [pw, q]
    args += [lw['ssm_d'], lw['glu_w'], lw['glu_b'], k, v, lw['w_out'], final_g]
    if not chained:
        args += [cinit, h0]
    in_specs = [tile] + [_vmem_spec() for _ in args[1:]]
    scratch = [
        pltpu.VMEM((rows, D), F32),
        pltpu.VMEM((rows, D), BF16),
        pltpu.VMEM((rows + 16, CONV_CHUNK), F32),
        pltpu.VMEM((rows, D), F32),
        pltpu.VMEM((rows, D), BF16),
        pltpu.VMEM((rows, D), F32),
        pltpu.VMEM((rows, D), F32),
        pltpu.VMEM((rows, D), BF16),
        pltpu.VMEM((rows, 3 * D), BF16),
    ]
    if chained:
        scratch += [pltpu.VMEM((16, D), F32), pltpu.VMEM((N_SLAB, N_SEG, D), F32)]
    return pl.pallas_call(
        functools.partial(_layer_kernel, ls=ls, chained=chained, final_norm=final_norm),
        grid=(nt,),
        in_specs=in_specs,
        out_specs=(tile,
                   pl.BlockSpec((2, N_SEG, D), lambda t: (0, 0, 0)),
                   pl.BlockSpec((N_SLAB, N_SEG, D), lambda t: (0, 0, 0))),
        out_shape=(jax.ShapeDtypeStruct(x3.shape, F32),
                   jax.ShapeDtypeStruct((2, N_SEG, D), F32),
                   jax.ShapeDtypeStruct((N_SLAB, N_SEG, D), F32)),
        scratch_shapes=scratch,
        compiler_params=pltpu.CompilerParams(dimension_semantics=("arbitrary",),
                                             vmem_limit_bytes=VMEM_LIMIT),
        name=name,
    )(*args)


def _state_from_slabs(st, seg):
    st = st[:, seg, :]
    n = st.shape[1]
    st_re = jnp.transpose(st[:, :, :SLAB_STATES], (1, 0, 2)).reshape(n, SSM_GROUPS, SSM_STATE)
    st_im = jnp.transpose(st[:, :, SLAB_STATES:], (1, 0, 2)).reshape(n, SSM_GROUPS, SSM_STATE)
    return st_re, st_im


def kernel(x_prompt, x_sample, mem_prompt, cache_conv, state_ssm_re, state_ssm_im, cache_mem_k, cache_mem_v, norm_g, w_in, conv_w, conv_b, ssm_lambda_re, ssm_lambda_im, ssm_log_dt, ssm_b_re, ssm_b_im, ssm_c_re, ssm_c_im, ssm_d, ssm_glu_w, ssm_glu_b, mem_norm_g, w_kv, w_out, final_norm_g):
    depth = w_in.shape[0]
    bp, seq, _ = x_prompt.shape
    bs, seq_s, _ = x_sample.shape
    assert bp == 1 and bs == N_SEG
    ls_p = 32
    assert seq % (N_SEG * ls_p) == 0 and seq_s % 2 == 0 and seq_s >= 2
    xp = x_prompt.reshape(seq // (N_SEG * ls_p), N_SEG, ls_p * D)
    xs = x_sample.reshape(1, N_SEG, seq_s * D)
    fg = final_norm_g.reshape(1, D)
    p_conv, p_re, p_im, p_k, p_v, s_conv, s_re, s_im = ([] for _ in range(8))
    for l in range(depth):
        last = l == depth - 1
        lw = dict(norm_g=norm_g[l].reshape(1, D), w_in=w_in[l].astype(BF16), conv_w=conv_w[l],
                  conv_b=conv_b[l].reshape(1, D), ssm_d=ssm_d[l].reshape(1, D),
                  glu_w=ssm_glu_w[l].astype(BF16), glu_b=ssm_glu_b[l].reshape(1, D),
                  w_out=w_out[l].astype(BF16))
        pw_re, pw_im, bb_re, bb_im = _discretize(ssm_lambda_re[l], ssm_lambda_im[l], ssm_log_dt[l],
                                                 ssm_b_re[l], ssm_b_im[l], ls_p)
        tables = _ssm_tables(pw_re, pw_im, bb_re, bb_im, ssm_c_re[l], ssm_c_im[l], ls_p)
        kp, vp = _mem_kv(mem_prompt[0], mem_norm_g[l], w_kv[l])
        xp, cv, st = _layer(xp, lw, tables, kp.astype(BF16), vp.astype(BF16), fg, ls=ls_p,
                            chained=True, final_norm=last, name=f"layer{l}_prompt")
        p_conv.append(cv[:, N_SEG - 1, :][None])
        st_re, st_im = _state_from_slabs(st, slice(N_SEG - 1, N_SEG))
        p_re.append(st_re)
        p_im.append(st_im)
        p_k.append(kp.reshape(1, N_MEM, HEADS, HEAD_DIM))
        p_v.append(vp.reshape(1, N_MEM, HEADS, HEAD_DIM))
        h0 = jnp.concatenate([jnp.transpose(_slab_cols(state_ssm_re[l]), (1, 0, 2)),
                              jnp.transpose(_slab_cols(state_ssm_im[l]), (1, 0, 2))], axis=-1)
        xs, cv, st = _layer(xs, lw, tables, cache_mem_k[l].reshape(bs * N_MEM, D).astype(BF16),
                            cache_mem_v[l].reshape(bs * N_MEM, D).astype(BF16), fg, ls=seq_s,
                            chained=False, final_norm=last,
                            cinit=jnp.transpose(cache_conv[l], (1, 0, 2)), h0=h0,
                            name=f"layer{l}_sample")
        s_conv.append(jnp.transpose(cv, (1, 0, 2)))
        st_re, st_im = _state_from_slabs(st, slice(0, N_SEG))
        s_re.append(st_re)
        s_im.append(st_im)
    return (xp.reshape(bp, seq, D), xs.reshape(bs, seq_s, D), jnp.stack(p_conv), jnp.stack(p_re),
            jnp.stack(p_im), jnp.stack(p_k), jnp.stack(p_v), jnp.stack(s_conv), jnp.stack(s_re),
            jnp.stack(s_im))
```

```python
import functools

import jax
import jax.numpy as jnp
from jax import lax
from jax.experimental import pallas as pl
from jax.experimental.pallas import tpu as pltpu

F32 = jnp.float32
BF16 = jnp.bfloat16

D = 1024
N_SEG = 8
SSM_GROUP = 16
SSM_STATE = 64
SSM_GROUPS = 64
N_SLAB = 8
SLAB = 128
SLAB_STATES = 512
HEADS = 4
HEAD_DIM = 256
N_MEM = 256
CONV_CHUNK = 512
EPS = 1e-6
MASK_VALUE = -1e30
VMEM_LIMIT = 60000 * 1024


def _dot(a, b):
    return jnp.dot(a, b, preferred_element_type=F32)


def _rms(x, g):
    ms = jnp.mean(x * x, axis=-1, keepdims=True)
    return x * lax.rsqrt(ms + EPS) * g


def _disc_kernel(lre_ref, lim_ref, ldt_ref, bre_ref, bim_ref,
                 pre_ref, pim_ref, bbre_ref, bbim_ref, *, ls):
    lr = lre_ref[...]
    li = lim_ref[...]
    dt = jnp.exp(ldt_ref[...])
    ea = jnp.exp(lr * dt)
    ar = ea * jnp.cos(li * dt)
    ai = ea * jnp.sin(li * dt)
    nr = ar - 1.0
    den = lr * lr + li * li
    fr = (nr * lr + ai * li) / den
    fi = (ai * lr - nr * li) / den
    for i in range(SSM_GROUP):
        br = bre_ref[i]
        bi = bim_ref[i]
        bbre_ref[i] = fr * br - fi * bi
        bbim_ref[i] = fr * bi + fi * br
    pr, pi = ar, ai
    for k in range(ls):
        pre_ref[k] = pr
        pim_ref[k] = pi
        if k + 1 < ls:
            pr, pi = pr * ar - pi * ai, pr * ai + pi * ar
    sr, si = pr, pi
    cr, ci = sr, si
    for m in range(N_SEG):
        pre_ref[ls + m] = cr
        pim_ref[ls + m] = ci
        cr, ci = cr * sr - ci * si, cr * si + ci * sr


def _discretize(lam_re, lam_im, log_dt, b_re, b_im, ls):
    g, n = lam_re.shape
    bt_re = jnp.transpose(b_re, (2, 0, 1))
    bt_im = jnp.transpose(b_im, (2, 0, 1))
    npow = ls + N_SEG
    return pl.pallas_call(
        functools.partial(_disc_kernel, ls=ls),
        out_shape=(jax.ShapeDtypeStruct((npow, g, n), F32),
                   jax.ShapeDtypeStruct((npow, g, n), F32),
                   jax.ShapeDtypeStruct((SSM_GROUP, g, n), F32),
                   jax.ShapeDtypeStruct((SSM_GROUP, g, n), F32)),
        name="ssm_discretize",
    )(lam_re, lam_im, log_dt.reshape(g, 1), bt_re, bt_im)


def _slab_cols(a):
    return a.reshape(a.shape[:-2] + (N_SLAB, SLAB_STATES))


def _ssm_tables(p_re, p_im, bb_re, bb_im, c_re, c_im):
    eye = jnp.eye(N_SEG, dtype=F32)

    def bblock(bb):
        b4 = jnp.transpose(bb, (1, 0, 2)).reshape(N_SLAB, 8, SSM_GROUP, SSM_STATE)
        return jnp.einsum('sgin,gh->sgihn', b4, eye).reshape(N_SLAB, SLAB, SLAB_STATES)

    def cblock(c):
        c4 = c.reshape(N_SLAB, 8, SSM_GROUP, SSM_STATE)
        return jnp.einsum('sgin,gh->sgnhi', c4, eye).reshape(N_SLAB, SLAB_STATES, SLAB)

    bblk = jnp.concatenate([bblock(bb_re), bblock(bb_im)], axis=-1).astype(BF16)
    cblk = jnp.concatenate([cblock(c_re), -cblock(c_im)], axis=1).astype(BF16)
    tab = jnp.transpose(jnp.concatenate([_slab_cols(p_re), _slab_cols(p_im)], axis=-1), (1, 0, 2))
    return bblk, cblk, tab


def _kv_kernel(mem_ref, g_ref, w_ref, k_ref, v_ref):
    h = _rms(mem_ref[...], g_ref[...]).astype(BF16)
    kv = _dot(h, w_ref[...].astype(BF16))
    k_ref[...] = kv[:, :D]
    v_ref[...] = kv[:, D:]


def _mem_kv(mem, g, w_kv):
    n = mem.shape[0]
    return pl.pallas_call(
        _kv_kernel,
        out_shape=(jax.ShapeDtypeStruct((n, D), F32), jax.ShapeDtypeStruct((n, D), F32)),
        compiler_params=pltpu.CompilerParams(vmem_limit_bytes=VMEM_LIMIT),
        name="mem_kv",
    )(mem, g.reshape(1, D), w_kv)


def _cmul(ar, ai, br, bi):
    return ar * br - ai * bi, ar * bi + ai * br


def _tile_copies(hbm, buf, sems, t, slot, to_hbm):
    cps = []
    for j in range(N_SEG):
        vm = buf.at[slot, :, j, :]
        hb = hbm.at[t, j]
        src, dst = (vm, hb) if to_hbm else (hb, vm)
        cps.append(pltpu.make_async_copy(src, dst, sems.at[slot, j]))
    return cps


def _layer_kernel(*refs, ls, chained, final_norm):
    rows = N_SEG * ls
    it = iter(refs)
    x_hbm, ng_ref, win_ref, cw_ref, cb_ref = (next(it) for _ in range(5))
    bblk_ref, cblk_ref, tab_ref = (next(it) for _ in range(3))
    dssm_ref, gw_ref, gb_ref, k_ref, v_ref, wout_ref, fg_ref = (next(it) for _ in range(7))
    if not chained:
        cinit_ref, h0_ref = next(it), next(it)
    o_hbm, convout_ref, ssmout_ref = (next(it) for _ in range(3))
    xbuf, obuf, sem_in, sem_out = (next(it) for _ in range(4))
    hbf, vext, bu, hb, su32, ys32, ysb, mix = (next(it) for _ in range(8))
    if chained:
        tail, carry = next(it), next(it)

    t = pl.program_id(0)
    nt = pl.num_programs(0)
    slot = t % 2

    @pl.when(t == 0)
    def _():
        for cp in _tile_copies(x_hbm, xbuf, sem_in, 0, 0, False):
            cp.start()
        if chained:
            tail[...] = jnp.zeros_like(tail)
            carry[...] = jnp.zeros_like(carry)

    @pl.when(t + 1 < nt)
    def _():
        for cp in _tile_copies(x_hbm, xbuf, sem_in, t + 1, 1 - slot, False):
            cp.start()

    for cp in _tile_copies(x_hbm, xbuf, sem_in, t, slot, False):
        cp.wait()

    xv = xbuf[slot].reshape(rows, D)
    hbf[...] = _rms(xv, ng_ref[...]).astype(BF16)

    sub_c = lax.broadcasted_iota(jnp.int32, (N_SEG, CONV_CHUNK), 0)
    for c in range(D // CONV_CHUNK):
        lo = c * CONV_CHUNK
        sl = slice(lo, lo + CONV_CHUNK)
        h = hbf[...]
        cx = _dot(h, win_ref[:, lo:lo + CONV_CHUNK])
        cc = _dot(h, win_ref[:, 2 * D + lo:2 * D + lo + CONV_CHUNK])
        v = cc * cx
        last2 = v[rows - 16:rows - 8]
        last1 = v[rows - 8:rows]
        if chained:
            b2 = pltpu.roll(jnp.where(sub_c == N_SEG - 1, tail[0:8, sl], last2), 1, 0)
            b1 = pltpu.roll(jnp.where(sub_c == N_SEG - 1, tail[8:16, sl], last1), 1, 0)
            tail[0:8, sl] = last2
            tail[8:16, sl] = last1
        else:
            b2 = cinit_ref[0, :, sl]
            b1 = cinit_ref[1, :, sl]
        convout_ref[0, :, sl] = last2
        convout_ref[1, :, sl] = last1
        vext[0:8] = b2
        vext[8:16] = b1
        vext[16:16 + rows] = v
        conv = (cw_ref[0:1, sl] * vext[0:rows] + cw_ref[1:2, sl] * vext[8:8 + rows]
                + cw_ref[2:3, sl] * vext[16:16 + rows] + cb_ref[:, sl])
        cb = _dot(h, win_ref[:, D + lo:D + lo + CONV_CHUNK])
        cg = _dot(h, win_ref[:, 3 * D + lo:3 * D + lo + CONV_CHUNK])
        mix[:, sl] = (cb * conv * jax.nn.silu(cg)).astype(BF16)

    su32[...] = _dot(hbf[...], win_ref[:, 4 * D:5 * D])
    sub_s = lax.broadcasted_iota(jnp.int32, (N_SEG, SLAB_STATES), 0)
    re = slice(0, SLAB_STATES)
    im = slice(SLAB_STATES, 2 * SLAB_STATES)
    blk = (N_SEG, SLAB_STATES)
    for s in range(N_SLAB):
        cs = slice(s * SLAB, (s + 1) * SLAB)
        bu[...] = _dot(su32[:, cs].astype(BF16), bblk_ref[s])
        lr = jnp.broadcast_to(tab_ref[s, 0:1, re], blk)
        li = jnp.broadcast_to(tab_ref[s, 0:1, im], blk)
        if chained:
            hr = jnp.zeros(blk, F32)
            hi = hr
        else:
            hr = h0_ref[s, :, re]
            hi = h0_ref[s, :, im]
        for i in range(ls):
            rw = slice(8 * i, 8 * i + 8)
            hr, hi = lr * hr - li * hi + bu[rw, re], lr * hi + li * hr + bu[rw, im]
            if chained:
                bu[rw, re] = hr
                bu[rw, im] = hi
            else:
                hb[rw, :] = jnp.concatenate([hr, hi], axis=1).astype(BF16)

        if chained:
            c0r = carry[s, 7:8, re]
            c0i = carry[s, 7:8, im]
            tr, ti = hr, hi
            for sh in (1, 2, 4):
                keep = sub_s >= sh
                qr = jnp.where(keep, tab_ref[s, ls + sh - 1:ls + sh, re], 0.0)
                qi = jnp.where(keep, tab_ref[s, ls + sh - 1:ls + sh, im], 0.0)
                dr, di = _cmul(qr, qi, pltpu.roll(tr, sh, 0), pltpu.roll(ti, sh, 0))
                tr, ti = tr + dr, ti + di
            dr, di = _cmul(tab_ref[s, ls:ls + N_SEG, re], tab_ref[s, ls:ls + N_SEG, im], c0r, c0i)
            tr, ti = tr + dr, ti + di
            carry[s, :, re] = tr
            carry[s, :, im] = ti
            ssmout_ref[s, :, re] = tr
            ssmout_ref[s, :, im] = ti
            sr = jnp.where(sub_s == 0, c0r, pltpu.roll(tr, 1, 0))
            si = jnp.where(sub_s == 0, c0i, pltpu.roll(ti, 1, 0))
            for i2 in range(ls // 2):
                blocks = []
                for i in (2 * i2, 2 * i2 + 1):
                    rw = slice(8 * i, 8 * i + 8)
                    dr, di = _cmul(tab_ref[s, i:i + 1, re], tab_ref[s, i:i + 1, im], sr, si)
                    blocks.append(jnp.concatenate([bu[rw, re] + dr, bu[rw, im] + di], axis=1))
                hb[16 * i2:16 * i2 + 16, :] = jnp.concatenate(blocks, axis=0).astype(BF16)
        else:
            ssmout_ref[s, :, re] = hr
            ssmout_ref[s, :, im] = hi

        y = _dot(hb[...], cblk_ref[s]) + dssm_ref[:, cs] * su32[:, cs]
        ys = jax.nn.gelu(y)
        ys32[:, cs] = ys
        ysb[:, cs] = ys.astype(BF16)

    gate = _dot(ysb[...], gw_ref[...]) + gb_ref[...]
    sg = _dot(hbf[...], win_ref[:, 5 * D:6 * D])
    mix[:, D:2 * D] = (ys32[...] * jax.nn.sigmoid(gate) * jax.nn.silu(sg)).astype(BF16)

    n_keys = k_ref.shape[0]
    if not chained:
        row_seq = lax.broadcasted_iota(jnp.int32, (rows, n_keys), 0) % N_SEG
        key_seq = lax.broadcasted_iota(jnp.int32, (rows, n_keys), 1) // N_MEM
        own = row_seq == key_seq
    scale = HEAD_DIM ** -0.5
    for hd in range(HEADS):
        hs = slice(hd * HEAD_DIM, (hd + 1) * HEAD_DIM)
        q = _dot(hbf[...], win_ref[:, 6 * D + hd * HEAD_DIM:6 * D + (hd + 1) * HEAD_DIM])
        sc = lax.dot_general(q.astype(BF16), k_ref[:, hs], (((1,), (1,)), ((), ())),
                             preferred_element_type=F32) * scale
        if not chained:
            sc = jnp.where(own, sc, MASK_VALUE)
        e = jnp.exp(sc - jnp.max(sc, axis=-1, keepdims=True))
        p = e / jnp.sum(e, axis=-1, keepdims=True)
        o = _dot(p.astype(BF16), v_ref[:, hs])
        qg = _dot(hbf[...], win_ref[:, 7 * D + hd * HEAD_DIM:7 * D + (hd + 1) * HEAD_DIM])
        mix[:, 2 * D + hd * HEAD_DIM:2 * D + (hd + 1) * HEAD_DIM] = (o * jax.nn.silu(qg)).astype(BF16)

    y = xbuf[slot].reshape(rows, D) + _dot(mix[...], wout_ref[...])
    if final_norm:
        y = _rms(y, fg_ref[...])

    @pl.when(t >= 2)
    def _():
        for cp in _tile_copies(o_hbm, obuf, sem_out, t - 2, slot, True):
            cp.wait()

    obuf[slot] = y.reshape(ls, N_SEG, D)
    for cp in _tile_copies(o_hbm, obuf, sem_out, t, slot, True):
        cp.start()

    @pl.when(t == nt - 1)
    def _():
        @pl.when(t >= 1)
        def _():
            for cp in _tile_copies(o_hbm, obuf, sem_out, t - 1, 1 - slot, True):
                cp.wait()
        for cp in _tile_copies(o_hbm, obuf, sem_out, t, slot, True):
            cp.wait()


def _vmem_spec():
    return pl.BlockSpec(memory_space=pltpu.VMEM)


def _layer(x4, lw, tables, k, v, final_g, *, chained, final_norm, cinit=None, h0=None, name):
    nt, _, ls, _ = x4.shape
    rows = N_SEG * ls
    bblk, cblk, tab = tables
    args = [x4, lw['norm_g'], lw['w_in'], lw['conv_w'], lw['conv_b'], bblk, cblk, tab,
            lw['ssm_d'], lw['glu_w'], lw['glu_b'], k, v, lw['w_out'], final_g]
    if not chained:
        args += [cinit, h0]
    in_specs = [pl.BlockSpec(memory_space=pl.ANY)] + [_vmem_spec() for _ in args[1:]]
    scratch = [
        pltpu.VMEM((2, ls, N_SEG, D), F32),
        pltpu.VMEM((2, ls, N_SEG, D), F32),
        pltpu.SemaphoreType.DMA((2, N_SEG)),
        pltpu.SemaphoreType.DMA((2, N_SEG)),
        pltpu.VMEM((rows, D), BF16),
        pltpu.VMEM((rows + 16, CONV_CHUNK), F32),
        pltpu.VMEM((rows, D), F32),
        pltpu.VMEM((rows, D), BF16),
        pltpu.VMEM((rows, D), F32),
        pltpu.VMEM((rows, D), F32),
        pltpu.VMEM((rows, D), BF16),
        pltpu.VMEM((rows, 3 * D), BF16),
    ]
    if chained:
        scratch += [pltpu.VMEM((16, D), F32), pltpu.VMEM((N_SLAB, N_SEG, D), F32)]
    return pl.pallas_call(
        functools.partial(_layer_kernel, ls=ls, chained=chained, final_norm=final_norm),
        grid=(nt,),
        in_specs=in_specs,
        out_specs=(pl.BlockSpec(memory_space=pl.ANY),
                   pl.BlockSpec((2, N_SEG, D), lambda t: (0, 0, 0)),
                   pl.BlockSpec((N_SLAB, N_SEG, D), lambda t: (0, 0, 0))),
        out_shape=(jax.ShapeDtypeStruct(x4.shape, F32),
                   jax.ShapeDtypeStruct((2, N_SEG, D), F32),
                   jax.ShapeDtypeStruct((N_SLAB, N_SEG, D), F32)),
        scratch_shapes=scratch,
        compiler_params=pltpu.CompilerParams(dimension_semantics=("arbitrary",),
                                             vmem_limit_bytes=VMEM_LIMIT),
        name=name,
    )(*args)


def _state_from_slabs(st):
    depth, _, n, _ = st.shape
    st = jnp.transpose(st, (0, 2, 1, 3))
    return (st[..., :SLAB_STATES].reshape(depth, n, SSM_GROUPS, SSM_STATE),
            st[..., SLAB_STATES:].reshape(depth, n, SSM_GROUPS, SSM_STATE))


def kernel(x_prompt, x_sample, mem_prompt, cache_conv, state_ssm_re, state_ssm_im, cache_mem_k, cache_mem_v, norm_g, w_in, conv_w, conv_b, ssm_lambda_re, ssm_lambda_im, ssm_log_dt, ssm_b_re, ssm_b_im, ssm_c_re, ssm_c_im, ssm_d, ssm_glu_w, ssm_glu_b, mem_norm_g, w_kv, w_out, final_norm_g):
    depth = w_in.shape[0]
    bp, seq, _ = x_prompt.shape
    bs, seq_s, _ = x_sample.shape
    assert bp == 1 and bs == N_SEG
    ls_p = 32
    assert seq % (N_SEG * ls_p) == 0 and seq_s % 8 == 0
    xp = x_prompt.reshape(seq // (N_SEG * ls_p), N_SEG, ls_p, D)
    xs = x_sample.reshape(1, N_SEG, seq_s, D)
    fg = final_norm_g.reshape(1, D)
    w_in_b, glu_b16, w_out_b = w_in.astype(BF16), ssm_glu_w.astype(BF16), w_out.astype(BF16)
    ks_b = cache_mem_k.reshape(depth, bs * N_MEM, D).astype(BF16)
    vs_b = cache_mem_v.reshape(depth, bs * N_MEM, D).astype(BF16)
    h0 = jnp.concatenate([jnp.transpose(_slab_cols(state_ssm_re), (0, 2, 1, 3)),
                          jnp.transpose(_slab_cols(state_ssm_im), (0, 2, 1, 3))], axis=-1)
    cinit = jnp.transpose(cache_conv, (0, 2, 1, 3))
    p_cv, p_st, p_k, p_v, s_cv, s_st = ([] for _ in range(6))
    for l in range(depth):
        last = l == depth - 1
        lw = dict(norm_g=norm_g[l].reshape(1, D), w_in=w_in_b[l], conv_w=conv_w[l],
                  conv_b=conv_b[l].reshape(1, D), ssm_d=ssm_d[l].reshape(1, D),
                  glu_w=glu_b16[l], glu_b=ssm_glu_b[l].reshape(1, D), w_out=w_out_b[l])
        pw_re, pw_im, bb_re, bb_im = _discretize(ssm_lambda_re[l], ssm_lambda_im[l], ssm_log_dt[l],
                                                 ssm_b_re[l], ssm_b_im[l], ls_p)
        tables = _ssm_tables(pw_re, pw_im, bb_re, bb_im, ssm_c_re[l], ssm_c_im[l])
        kp, vp = _mem_kv(mem_prompt[0], mem_norm_g[l], w_kv[l])
        xp, cv, st = _layer(xp, lw, tables, kp.astype(BF16), vp.astype(BF16), fg,
                            chained=True, final_norm=last, name=f"layer{l}_prompt")
        p_cv.append(cv)
        p_st.append(st)
        p_k.append(kp)
        p_v.append(vp)
        xs, cv, st = _layer(xs, lw, tables, ks_b[l], vs_b[l], fg, chained=False, final_norm=last,
                            cinit=cinit[l], h0=h0[l], name=f"layer{l}_sample")
        s_cv.append(cv)
        s_st.append(st)
    p_cv, p_st, s_cv, s_st = (jnp.stack(a) for a in (p_cv, p_st, s_cv, s_st))
    p_re, p_im = _state_from_slabs(p_st[:, :, N_SEG - 1:, :])
    s_re, s_im = _state_from_slabs(s_st)
    kv_shape = (depth, 1, N_MEM, HEADS, HEAD_DIM)
    return (xp.reshape(bp, seq, D), xs.reshape(bs, seq_s, D),
            p_cv[:, :, N_SEG - 1, :][:, None], p_re, p_im,
            jnp.stack(p_k).reshape(kv_shape), jnp.stack(p_v).reshape(kv_shape),
            jnp.transpose(s_cv, (0, 2, 1, 3)), s_re, s_im)
```

```python
import functools

import jax
import jax.numpy as jnp
from jax import lax
from jax.experimental import pallas as pl
from jax.experimental.pallas import tpu as pltpu

F32 = jnp.float32
BF16 = jnp.bfloat16

D = 1024
N_SEG = 8
SSM_GROUP = 16
SSM_STATE = 64
SSM_GROUPS = 64
N_SLAB = 8
SLAB = 128
SLAB_STATES = 512
HEADS = 4
HEAD_DIM = 256
N_MEM = 256
CONV_CHUNK = 512
EPS = 1e-6
MASK_VALUE = -1e30
VMEM_LIMIT = 60000 * 1024


def _dot(a, b):
    return jnp.dot(a, b, preferred_element_type=F32)


def _rms(x, g):
    ms = jnp.mean(x * x, axis=-1, keepdims=True)
    return x * lax.rsqrt(ms + EPS) * g


def _disc_kernel(lre_ref, lim_ref, ldt_ref, bre_ref, bim_ref,
                 pre_ref, pim_ref, bbre_ref, bbim_ref, *, ls):
    lr = lre_ref[...]
    li = lim_ref[...]
    dt = jnp.exp(ldt_ref[...])
    ea = jnp.exp(lr * dt)
    ar = ea * jnp.cos(li * dt)
    ai = ea * jnp.sin(li * dt)
    nr = ar - 1.0
    den = lr * lr + li * li
    fr = (nr * lr + ai * li) / den
    fi = (ai * lr - nr * li) / den
    for i in range(SSM_GROUP):
        br = bre_ref[i]
        bi = bim_ref[i]
        bbre_ref[i] = fr * br - fi * bi
        bbim_ref[i] = fr * bi + fi * br
    pr, pi = ar, ai
    for k in range(ls):
        pre_ref[k] = pr
        pim_ref[k] = pi
        if k + 1 < ls:
            pr, pi = pr * ar - pi * ai, pr * ai + pi * ar
    sr, si = pr, pi
    cr, ci = sr, si
    for m in range(N_SEG):
        pre_ref[ls + m] = cr
        pim_ref[ls + m] = ci
        cr, ci = cr * sr - ci * si, cr * si + ci * sr


def _discretize(lam_re, lam_im, log_dt, b_re, b_im, ls):
    g, n = lam_re.shape
    bt_re = jnp.transpose(b_re, (2, 0, 1))
    bt_im = jnp.transpose(b_im, (2, 0, 1))
    npow = ls + N_SEG
    return pl.pallas_call(
        functools.partial(_disc_kernel, ls=ls),
        out_shape=(jax.ShapeDtypeStruct((npow, g, n), F32),
                   jax.ShapeDtypeStruct((npow, g, n), F32),
                   jax.ShapeDtypeStruct((SSM_GROUP, g, n), F32),
                   jax.ShapeDtypeStruct((SSM_GROUP, g, n), F32)),
        name="ssm_discretize",
    )(lam_re, lam_im, log_dt.reshape(g, 1), bt_re, bt_im)


def _slab_cols(a):
    return a.reshape(a.shape[:-2] + (N_SLAB, SLAB_STATES))


def _ssm_tables(p_re, p_im, bb_re, bb_im, c_re, c_im):
    eye = jnp.eye(N_SEG, dtype=F32)

    def bblock(bb):
        b4 = jnp.transpose(bb, (1, 0, 2)).reshape(N_SLAB, 8, SSM_GROUP, SSM_STATE)
        return jnp.einsum('sgin,gh->sgihn', b4, eye).reshape(N_SLAB, SLAB, SLAB_STATES)

    def cblock(c):
        c4 = c.reshape(N_SLAB, 8, SSM_GROUP, SSM_STATE)
        return jnp.einsum('sgin,gh->sgnhi', c4, eye).reshape(N_SLAB, SLAB_STATES, SLAB)

    bblk = jnp.concatenate([bblock(bb_re), bblock(bb_im)], axis=-1).astype(BF16)
    cblk = jnp.concatenate([cblock(c_re), -cblock(c_im)], axis=1).astype(BF16)
    tab = jnp.transpose(jnp.concatenate([_slab_cols(p_re), _slab_cols(p_im)], axis=-1), (1, 0, 2))
    return bblk, cblk, tab


def _kv_kernel(mem_ref, g_ref, w_ref, k_ref, v_ref):
    h = _rms(mem_ref[...], g_ref[...]).astype(BF16)
    kv = _dot(h, w_ref[...].astype(BF16))
    k_ref[...] = kv[:, :D]
    v_ref[...] = kv[:, D:]


def _mem_kv(mem, g, w_kv):
    n = mem.shape[0]
    return pl.pallas_call(
        _kv_kernel,
        out_shape=(jax.ShapeDtypeStruct((n, D), F32), jax.ShapeDtypeStruct((n, D), F32)),
        compiler_params=pltpu.CompilerParams(vmem_limit_bytes=VMEM_LIMIT),
        name="mem_kv",
    )(mem, g.reshape(1, D), w_kv)


def _cmul(ar, ai, br, bi):
    return ar * br - ai * bi, ar * bi + ai * br


def _tile_copies(hbm, buf, sems, t, slot, to_hbm):
    cps = []
    for j in range(N_SEG):
        vm = buf.at[slot, :, j, :]
        hb = hbm.at[t, j]
        src, dst = (vm, hb) if to_hbm else (hb, vm)
        cps.append(pltpu.make_async_copy(src, dst, sems.at[slot, j]))
    return cps


def _layer_kernel(*refs, ls, chained, final_norm):
    rows = N_SEG * ls
    it = iter(refs)
    x_hbm, win_hbm, gw_hbm, wout_hbm, bblk_hbm, cblk_hbm, k_hbm, v_hbm = (next(it) for _ in range(8))
    ng_ref, cw_ref, cb_ref, tab_ref, dssm_ref, gb_ref, fg_ref = (next(it) for _ in range(7))
    if not chained:
        cinit_ref, h0_ref = next(it), next(it)
    o_hbm, convout_ref, ssmout_ref = (next(it) for _ in range(3))
    xbuf, obuf, sem_in, sem_out = (next(it) for _ in range(4))
    win_ref, gw_ref, wout_ref, bblk_ref, cblk_ref, k_ref, v_ref, sem_w = (next(it) for _ in range(8))
    hbf, vext, bus, hbs, su32, ys32, ysb, mix, sgs, oacc = (next(it) for _ in range(10))
    if chained:
        tail, carry = next(it), next(it)

    t = pl.program_id(0)
    nt = pl.num_programs(0)
    slot = t % 2
    resident = [pltpu.make_async_copy(src, dst, sem_w.at[n]) for n, (src, dst) in enumerate(
        ((win_hbm, win_ref), (gw_hbm, gw_ref), (wout_hbm, wout_ref), (bblk_hbm, bblk_ref),
         (cblk_hbm, cblk_ref), (k_hbm, k_ref), (v_hbm, v_ref)))]

    @pl.when(t == 0)
    def _():
        for cp in resident:
            cp.start()
        for cp in _tile_copies(x_hbm, xbuf, sem_in, 0, 0, False):
            cp.start()
        if chained:
            tail[...] = jnp.zeros_like(tail)
            carry[...] = jnp.zeros_like(carry)

    @pl.when(t + 1 < nt)
    def _():
        for cp in _tile_copies(x_hbm, xbuf, sem_in, t + 1, 1 - slot, False):
            cp.start()

    for cp in _tile_copies(x_hbm, xbuf, sem_in, t, slot, False):
        cp.wait()

    @pl.when(t == 0)
    def _():
        for cp in resident:
            cp.wait()

    xv = xbuf[slot].reshape(rows, D)
    hbf[...] = _rms(xv, ng_ref[...]).astype(BF16)

    re = slice(0, SLAB_STATES)
    im = slice(SLAB_STATES, 2 * SLAB_STATES)
    blk = (N_SEG, SLAB_STATES)
    sub_c = lax.broadcasted_iota(jnp.int32, (N_SEG, CONV_CHUNK), 0)
    sub_s = lax.broadcasted_iota(jnp.int32, blk, 0)

    def conv_chunk(c):
        lo = c * CONV_CHUNK
        sl = slice(lo, lo + CONV_CHUNK)
        h = hbf[...]
        cx = _dot(h, win_ref[:, lo:lo + CONV_CHUNK])
        cc = _dot(h, win_ref[:, 2 * D + lo:2 * D + lo + CONV_CHUNK])
        v = cc * cx
        last2 = v[rows - 16:rows - 8]
        last1 = v[rows - 8:rows]
        if chained:
            b2 = pltpu.roll(jnp.where(sub_c == N_SEG - 1, tail[0:8, sl], last2), 1, 0)
            b1 = pltpu.roll(jnp.where(sub_c == N_SEG - 1, tail[8:16, sl], last1), 1, 0)
            tail[0:8, sl] = last2
            tail[8:16, sl] = last1
        else:
            b2 = cinit_ref[0, :, sl]
            b1 = cinit_ref[1, :, sl]
        convout_ref[0, :, sl] = last2
        convout_ref[1, :, sl] = last1
        vx = vext.at[c]
        vx[0:8] = b2
        vx[8:16] = b1
        vx[16:16 + rows] = v
        conv = (cw_ref[0:1, sl] * vx[0:rows] + cw_ref[1:2, sl] * vx[8:8 + rows]
                + cw_ref[2:3, sl] * vx[16:16 + rows] + cb_ref[:, sl])
        cb = _dot(h, win_ref[:, D + lo:D + lo + CONV_CHUNK])
        cg = _dot(h, win_ref[:, 3 * D + lo:3 * D + lo + CONV_CHUNK])
        mix[:, sl] = (cb * conv * jax.nn.silu(cg)).astype(BF16)

    n_keys = k_ref.shape[0]
    if not chained:
        row_seq = lax.broadcasted_iota(jnp.int32, (rows, n_keys), 0) % N_SEG
        key_seq = lax.broadcasted_iota(jnp.int32, (rows, n_keys), 1) // N_MEM
        own = row_seq == key_seq
    scale = HEAD_DIM ** -0.5

    def attn_head(hd):
        hs = slice(hd * HEAD_DIM, (hd + 1) * HEAD_DIM)
        q = _dot(hbf[...], win_ref[:, 6 * D + hd * HEAD_DIM:6 * D + (hd + 1) * HEAD_DIM])
        sc = lax.dot_general(q.astype(BF16), k_ref[:, hs], (((1,), (1,)), ((), ())),
                             preferred_element_type=F32) * scale
        if not chained:
            sc = jnp.where(own, sc, MASK_VALUE)
        e = jnp.exp(sc - jnp.max(sc, axis=-1, keepdims=True))
        p = e / jnp.sum(e, axis=-1, keepdims=True)
        o = _dot(p.astype(BF16), v_ref[:, hs])
        qg = _dot(hbf[...], win_ref[:, 7 * D + hd * HEAD_DIM:7 * D + (hd + 1) * HEAD_DIM])
        mix[:, 2 * D + hd * HEAD_DIM:2 * D + (hd + 1) * HEAD_DIM] = (o * jax.nn.silu(qg)).astype(BF16)

    def ssm_gate_proj():
        sgs[...] = _dot(hbf[...], win_ref[:, 5 * D:6 * D])

    def out_proj_conv_attn():
        oacc[...] = (_dot(mix[:, 0:D], wout_ref[0:D, :])
                     + _dot(mix[:, 2 * D:3 * D], wout_ref[2 * D:3 * D, :]))

    between = {0: lambda: conv_chunk(0), 1: lambda: conv_chunk(1), 2: ssm_gate_proj,
               3: lambda: attn_head(0), 4: lambda: attn_head(1), 5: lambda: attn_head(2),
               6: lambda: attn_head(3), 7: out_proj_conv_attn}

    su32[...] = _dot(hbf[...], win_ref[:, 4 * D:5 * D])
    for s in range(N_SLAB):
        cs = slice(s * SLAB, (s + 1) * SLAB)
        bu, hb = bus.at[s % 2], hbs.at[s % 2]
        bu[...] = _dot(su32[:, cs].astype(BF16), bblk_ref[s])
        between[s]()
        lr = jnp.broadcast_to(tab_ref[s, 0:1, re], blk)
        li = jnp.broadcast_to(tab_ref[s, 0:1, im], blk)
        if chained:
            hr = jnp.zeros(blk, F32)
            hi = hr
        else:
            hr = h0_ref[s, :, re]
            hi = h0_ref[s, :, im]
        for i in range(ls):
            rw = slice(8 * i, 8 * i + 8)
            hr, hi = lr * hr - li * hi + bu[rw, re], lr * hi + li * hr + bu[rw, im]
            if chained:
                bu[rw, re] = hr
                bu[rw, im] = hi
            else:
                hb[rw, :] = jnp.concatenate([hr, hi], axis=1).astype(BF16)

        if chained:
            c0r = carry[s, 7:8, re]
            c0i = carry[s, 7:8, im]
            tr, ti = hr, hi
            for sh in (1, 2, 4):
                keep = sub_s >= sh
                qr = jnp.where(keep, tab_ref[s, ls + sh - 1:ls + sh, re], 0.0)
                qi = jnp.where(keep, tab_ref[s, ls + sh - 1:ls + sh, im], 0.0)
                dr, di = _cmul(qr, qi, pltpu.roll(tr, sh, 0), pltpu.roll(ti, sh, 0))
                tr, ti = tr + dr, ti + di
            dr, di = _cmul(tab_ref[s, ls:ls + N_SEG, re], tab_ref[s, ls:ls + N_SEG, im], c0r, c0i)
            tr, ti = tr + dr, ti + di
            carry[s, :, re] = tr
            carry[s, :, im] = ti
            ssmout_ref[s, :, re] = tr
            ssmout_ref[s, :, im] = ti
            sr = jnp.where(sub_s == 0, c0r, pltpu.roll(tr, 1, 0))
            si = jnp.where(sub_s == 0, c0i, pltpu.roll(ti, 1, 0))
            for i2 in range(ls // 2):
                blocks = []
                for i in (2 * i2, 2 * i2 + 1):
                    rw = slice(8 * i, 8 * i + 8)
                    dr, di = _cmul(tab_ref[s, i:i + 1, re], tab_ref[s, i:i + 1, im], sr, si)
                    blocks.append(jnp.concatenate([bu[rw, re] + dr, bu[rw, im] + di], axis=1))
                hb[16 * i2:16 * i2 + 16, :] = jnp.concatenate(blocks, axis=0).astype(BF16)
        else:
            ssmout_ref[s, :, re] = hr
            ssmout_ref[s, :, im] = hi

        y = _dot(hb[...], cblk_ref[s]) + dssm_ref[:, cs] * su32[:, cs]
        ys = jax.nn.gelu(y)
        ys32[:, cs] = ys
        ysb[:, cs] = ys.astype(BF16)

    gate = _dot(ysb[...], gw_ref[...]) + gb_ref[...]
    mix[:, D:2 * D] = (ys32[...] * jax.nn.sigmoid(gate) * jax.nn.silu(sgs[...])).astype(BF16)

    y = xbuf[slot].reshape(rows, D) + (oacc[...] + _dot(mix[:, D:2 * D], wout_ref[D:2 * D, :]))
    if final_norm:
        y = _rms(y, fg_ref[...])

    @pl.when(t >= 2)
    def _():
        for cp in _tile_copies(o_hbm, obuf, sem_out, t - 2, slot, True):
            cp.wait()

    obuf[slot] = y.reshape(ls, N_SEG, D)
    for cp in _tile_copies(o_hbm, obuf, sem_out, t, slot, True):
        cp.start()

    @pl.when(t == nt - 1)
    def _():
        @pl.when(t >= 1)
        def _():
            for cp in _tile_copies(o_hbm, obuf, sem_out, t - 1, 1 - slot, True):
                cp.wait()
        for cp in _tile_copies(o_hbm, obuf, sem_out, t, slot, True):
            cp.wait()


def _vmem_spec():
    return pl.BlockSpec(memory_space=pltpu.VMEM)


def _layer(x4, lw, tables, k, v, final_g, *, chained, final_norm, cinit=None, h0=None, name):
    nt, _, ls, _ = x4.shape
    rows = N_SEG * ls
    bblk, cblk, tab = tables
    in_hbm = [x4, lw['w_in'], lw['glu_w'], lw['w_out'], bblk, cblk, k, v]
    in_vmem = [lw['norm_g'], lw['conv_w'], lw['conv_b'], tab, lw['ssm_d'], lw['glu_b'], final_g]
    if not chained:
        in_vmem += [cinit, h0]
    args = in_hbm + in_vmem
    in_specs = [pl.BlockSpec(memory_space=pl.ANY) for _ in in_hbm] + [_vmem_spec() for _ in in_vmem]
    scratch = [
        pltpu.VMEM((2, ls, N_SEG, D), F32),
        pltpu.VMEM((2, ls, N_SEG, D), F32),
        pltpu.SemaphoreType.DMA((2, N_SEG)),
        pltpu.SemaphoreType.DMA((2, N_SEG)),
    ] + [pltpu.VMEM(a.shape, a.dtype) for a in in_hbm[1:]] + [
        pltpu.SemaphoreType.DMA((len(in_hbm) - 1,)),
        pltpu.VMEM((rows, D), BF16),
        pltpu.VMEM((D // CONV_CHUNK, rows + 16, CONV_CHUNK), F32),
        pltpu.VMEM((2, rows, D), F32),
        pltpu.VMEM((2, rows, D), BF16),
        pltpu.VMEM((rows, D), F32),
        pltpu.VMEM((rows, D), F32),
        pltpu.VMEM((rows, D), BF16),
        pltpu.VMEM((rows, 3 * D), BF16),
        pltpu.VMEM((rows, D), F32),
        pltpu.VMEM((rows, D), F32),
    ]
    if chained:
        scratch += [pltpu.VMEM((16, D), F32), pltpu.VMEM((N_SLAB, N_SEG, D), F32)]
    return pl.pallas_call(
        functools.partial(_layer_kernel, ls=ls, chained=chained, final_norm=final_norm),
        grid=(nt,),
        in_specs=in_specs,
        out_specs=(pl.BlockSpec(memory_space=pl.ANY),
                   pl.BlockSpec((2, N_SEG, D), lambda t: (0, 0, 0)),
                   pl.BlockSpec((N_SLAB, N_SEG, D), lambda t: (0, 0, 0))),
        out_shape=(jax.ShapeDtypeStruct(x4.shape, F32),
                   jax.ShapeDtypeStruct((2, N_SEG, D), F32),
                   jax.ShapeDtypeStruct((N_SLAB, N_SEG, D), F32)),
        scratch_shapes=scratch,
        compiler_params=pltpu.CompilerParams(dimension_semantics=("arbitrary",),
                                             vmem_limit_bytes=VMEM_LIMIT),
        name=name,
    )(*args)


def _state_from_slabs(st):
    depth, _, n, _ = st.shape
    st = jnp.transpose(st, (0, 2, 1, 3))
    return (st[..., :SLAB_STATES].reshape(depth, n, SSM_GROUPS, SSM_STATE),
            st[..., SLAB_STATES:].reshape(depth, n, SSM_GROUPS, SSM_STATE))


def kernel(x_prompt, x_sample, mem_prompt, cache_conv, state_ssm_re, state_ssm_im, cache_mem_k, cache_mem_v, norm_g, w_in, conv_w, conv_b, ssm_lambda_re, ssm_lambda_im, ssm_log_dt, ssm_b_re, ssm_b_im, ssm_c_re, ssm_c_im, ssm_d, ssm_glu_w, ssm_glu_b, mem_norm_g, w_kv, w_out, final_norm_g):
    depth = w_in.shape[0]
    bp, seq, _ = x_prompt.shape
    bs, seq_s, _ = x_sample.shape
    assert bp == 1 and bs == N_SEG
    ls_p = 32
    assert seq % (N_SEG * ls_p) == 0 and seq_s % 8 == 0
    xp = x_prompt.reshape(seq // (N_SEG * ls_p), N_SEG, ls_p, D)
    xs = x_sample.reshape(1, N_SEG, seq_s, D)
    fg = final_norm_g.reshape(1, D)
    w_in_b, glu_b16, w_out_b = w_in.astype(BF16), ssm_glu_w.astype(BF16), w_out.astype(BF16)
    ks_b = cache_mem_k.reshape(depth, bs * N_MEM, D).astype(BF16)
    vs_b = cache_mem_v.reshape(depth, bs * N_MEM, D).astype(BF16)
    h0 = jnp.concatenate([jnp.transpose(_slab_cols(state_ssm_re), (0, 2, 1, 3)),
                          jnp.transpose(_slab_cols(state_ssm_im), (0, 2, 1, 3))], axis=-1)
    cinit = jnp.transpose(cache_conv, (0, 2, 1, 3))
    p_cv, p_st, p_k, p_v, s_cv, s_st = ([] for _ in range(6))
    for l in range(depth):
        last = l == depth - 1
        lw = dict(norm_g=norm_g[l].reshape(1, D), w_in=w_in_b[l], conv_w=conv_w[l],
                  conv_b=conv_b[l].reshape(1, D), ssm_d=ssm_d[l].reshape(1, D),
                  glu_w=glu_b16[l], glu_b=ssm_glu_b[l].reshape(1, D), w_out=w_out_b[l])
        pw_re, pw_im, bb_re, bb_im = _discretize(ssm_lambda_re[l], ssm_lambda_im[l], ssm_log_dt[l],
                                                 ssm_b_re[l], ssm_b_im[l], ls_p)
        tables = _ssm_tables(pw_re, pw_im, bb_re, bb_im, ssm_c_re[l], ssm_c_im[l])
        kp, vp = _mem_kv(mem_prompt[0], mem_norm_g[l], w_kv[l])
        xp, cv, st = _layer(xp, lw, tables, kp.astype(BF16), vp.astype(BF16), fg,
                            chained=True, final_norm=last, name=f"layer{l}_prompt")
        p_cv.append(cv)
        p_st.append(st)
        p_k.append(kp)
        p_v.append(vp)
        xs, cv, st = _layer(xs, lw, tables, ks_b[l], vs_b[l], fg, chained=False, final_norm=last,
                            cinit=cinit[l], h0=h0[l], name=f"layer{l}_sample")
        s_cv.append(cv)
        s_st.append(st)
    p_cv, p_st, s_cv, s_st = (jnp.stack(a) for a in (p_cv, p_st, s_cv, s_st))
    p_re, p_im = _state_from_slabs(p_st[:, :, N_SEG - 1:, :])
    s_re, s_im = _state_from_slabs(s_st)
    kv_shape = (depth, 1, N_MEM, HEADS, HEAD_DIM)
    return (xp.reshape(bp, seq, D), xs.reshape(bs, seq_s, D),
            p_cv[:, :, N_SEG - 1, :][:, None], p_re, p_im,
            jnp.stack(p_k).reshape(kv_shape), jnp.stack(p_v).reshape(kv_shape),
            jnp.transpose(s_cv, (0, 2, 1, 3)), s_re, s_im)
```

```python
import functools

import jax
import jax.numpy as jnp
from jax import lax
from jax.experimental import pallas as pl
from jax.experimental.pallas import tpu as pltpu

F32 = jnp.float32
BF16 = jnp.bfloat16

D = 1024
N_SEG = 8
SSM_GROUP = 16
SSM_STATE = 64
SSM_GROUPS = 64
N_SLAB = 8
SLAB = 128
SLAB_STATES = 512
HEADS = 4
HEAD_DIM = 256
N_MEM = 256
CONV_CHUNK = 512
EPS = 1e-6
MASK_VALUE = -1e30
VMEM_LIMIT = 60000 * 1024


def _dot(a, b):
    return jnp.dot(a, b, preferred_element_type=F32)


def _rms(x, g):
    ms = jnp.mean(x * x, axis=-1, keepdims=True)
    return x * lax.rsqrt(ms + EPS) * g


def _disc_kernel(lre_ref, lim_ref, ldt_ref, bre_ref, bim_ref,
                 pre_ref, pim_ref, bbre_ref, bbim_ref, *, ls):
    lr = lre_ref[...]
    li = lim_ref[...]
    dt = jnp.exp(ldt_ref[...])
    ea = jnp.exp(lr * dt)
    ar = ea * jnp.cos(li * dt)
    ai = ea * jnp.sin(li * dt)
    nr = ar - 1.0
    den = lr * lr + li * li
    fr = (nr * lr + ai * li) / den
    fi = (ai * lr - nr * li) / den
    for i in range(SSM_GROUP):
        br = bre_ref[i]
        bi = bim_ref[i]
        bbre_ref[i] = fr * br - fi * bi
        bbim_ref[i] = fr * bi + fi * br
    pre_ref[0] = ar
    pim_ref[0] = ai
    sr, si = ar, ai
    for _ in range(ls - 1):
        sr, si = sr * ar - si * ai, sr * ai + si * ar
    cr, ci = sr, si
    for m in range(1, N_SEG + 1):
        pre_ref[m] = cr
        pim_ref[m] = ci
        cr, ci = cr * sr - ci * si, cr * si + ci * sr


def _discretize(lam_re, lam_im, log_dt, b_re, b_im, ls):
    g, n = lam_re.shape
    bt_re = jnp.transpose(b_re, (2, 0, 1))
    bt_im = jnp.transpose(b_im, (2, 0, 1))
    npow = 1 + N_SEG
    return pl.pallas_call(
        functools.partial(_disc_kernel, ls=ls),
        out_shape=(jax.ShapeDtypeStruct((npow, g, n), F32),
                   jax.ShapeDtypeStruct((npow, g, n), F32),
                   jax.ShapeDtypeStruct((SSM_GROUP, g, n), F32),
                   jax.ShapeDtypeStruct((SSM_GROUP, g, n), F32)),
        name="ssm_discretize",
    )(lam_re, lam_im, log_dt.reshape(g, 1), bt_re, bt_im)


def _slab_cols(a):
    return a.reshape(a.shape[:-2] + (N_SLAB, SLAB_STATES))


def _ssm_tables(p_re, p_im, bb_re, bb_im, c_re, c_im):
    eye = jnp.eye(N_SEG, dtype=F32)

    def bblock(bb):
        b4 = jnp.transpose(bb, (1, 0, 2)).reshape(N_SLAB, 8, SSM_GROUP, SSM_STATE)
        return jnp.einsum('sgin,gh->sgihn', b4, eye).reshape(N_SLAB, SLAB, SLAB_STATES)

    def cblock(c):
        c4 = c.reshape(N_SLAB, 8, SSM_GROUP, SSM_STATE)
        return jnp.einsum('sgin,gh->sgnhi', c4, eye).reshape(N_SLAB, SLAB_STATES, SLAB)

    bblk = jnp.concatenate([bblock(bb_re), bblock(bb_im)], axis=-1).astype(BF16)
    cblk = jnp.concatenate([cblock(c_re), -cblock(c_im)], axis=1).astype(BF16)
    tab = jnp.transpose(jnp.concatenate([_slab_cols(p_re), _slab_cols(p_im)], axis=-1), (1, 0, 2))
    return bblk, cblk, tab


def _kv_kernel(mem_ref, g_ref, w_ref, k_ref, v_ref):
    h = _rms(mem_ref[...], g_ref[...]).astype(BF16)
    kv = _dot(h, w_ref[...].astype(BF16))
    k_ref[...] = kv[:, :D]
    v_ref[...] = kv[:, D:]


def _mem_kv(mem, g, w_kv):
    n = mem.shape[0]
    return pl.pallas_call(
        _kv_kernel,
        out_shape=(jax.ShapeDtypeStruct((n, D), F32), jax.ShapeDtypeStruct((n, D), F32)),
        compiler_params=pltpu.CompilerParams(vmem_limit_bytes=VMEM_LIMIT),
        name="mem_kv",
    )(mem, g.reshape(1, D), w_kv)


def _cmul(ar, ai, br, bi):
    return ar * br - ai * bi, ar * bi + ai * br


def _tile_copies(hbm, buf, sems, t, slot, to_hbm):
    cps = []
    for j in range(N_SEG):
        vm = buf.at[slot, :, j, :]
        hb = hbm.at[t, j]
        src, dst = (vm, hb) if to_hbm else (hb, vm)
        cps.append(pltpu.make_async_copy(src, dst, sems.at[slot, j]))
    return cps


def _layer_kernel(*refs, ls, chained, final_norm):
    rows = N_SEG * ls
    it = iter(refs)
    x_hbm, win_hbm, gw_hbm, wout_hbm, bblk_hbm, cblk_hbm, k_hbm, v_hbm = (next(it) for _ in range(8))
    ng_ref, cw_ref, cb_ref, tab_ref, dssm_ref, gb_ref, fg_ref = (next(it) for _ in range(7))
    if not chained:
        cinit_ref, h0_ref = next(it), next(it)
    o_hbm, convout_ref, ssmout_ref = (next(it) for _ in range(3))
    xbuf, obuf, sem_in, sem_out = (next(it) for _ in range(4))
    win_ref, gw_ref, wout_ref, bblk_ref, cblk_ref, k_ref, v_ref, sem_w = (next(it) for _ in range(8))
    hbf, vext, bus, hbs, su32, ys32, ysb, mix, sgs, oacc = (next(it) for _ in range(10))
    if chained:
        tail, carry = next(it), next(it)

    t = pl.program_id(0)
    nt = pl.num_programs(0)
    slot = t % 2
    resident = [pltpu.make_async_copy(src, dst, sem_w.at[n]) for n, (src, dst) in enumerate(
        ((win_hbm, win_ref), (gw_hbm, gw_ref), (wout_hbm, wout_ref), (bblk_hbm, bblk_ref),
         (cblk_hbm, cblk_ref), (k_hbm, k_ref), (v_hbm, v_ref)))]

    @pl.when(t == 0)
    def _():
        for cp in resident:
            cp.start()
        for cp in _tile_copies(x_hbm, xbuf, sem_in, 0, 0, False):
            cp.start()
        if chained:
            tail[...] = jnp.zeros_like(tail)
            carry[...] = jnp.zeros_like(carry)

    @pl.when(t + 1 < nt)
    def _():
        for cp in _tile_copies(x_hbm, xbuf, sem_in, t + 1, 1 - slot, False):
            cp.start()

    for cp in _tile_copies(x_hbm, xbuf, sem_in, t, slot, False):
        cp.wait()

    @pl.when(t == 0)
    def _():
        for cp in resident:
            cp.wait()

    xv = xbuf[slot].reshape(rows, D)
    hbf[...] = _rms(xv, ng_ref[...]).astype(BF16)

    re = slice(0, SLAB_STATES)
    im = slice(SLAB_STATES, 2 * SLAB_STATES)
    blk = (N_SEG, SLAB_STATES)
    sub_c = lax.broadcasted_iota(jnp.int32, (N_SEG, CONV_CHUNK), 0)
    sub_s = lax.broadcasted_iota(jnp.int32, blk, 0)

    def conv_chunk(c):
        lo = c * CONV_CHUNK
        sl = slice(lo, lo + CONV_CHUNK)
        h = hbf[...]
        cx = _dot(h, win_ref[:, lo:lo + CONV_CHUNK])
        cc = _dot(h, win_ref[:, 2 * D + lo:2 * D + lo + CONV_CHUNK])
        v = cc * cx
        last2 = v[rows - 16:rows - 8]
        last1 = v[rows - 8:rows]
        if chained:
            b2 = pltpu.roll(jnp.where(sub_c == N_SEG - 1, tail[0:8, sl], last2), 1, 0)
            b1 = pltpu.roll(jnp.where(sub_c == N_SEG - 1, tail[8:16, sl], last1), 1, 0)
            tail[0:8, sl] = last2
            tail[8:16, sl] = last1
        else:
            b2 = cinit_ref[0, :, sl]
            b1 = cinit_ref[1, :, sl]
        convout_ref[0, :, sl] = last2
        convout_ref[1, :, sl] = last1
        vx = vext.at[c]
        vx[0:8] = b2
        vx[8:16] = b1
        vx[16:16 + rows] = v
        conv = (cw_ref[0:1, sl] * vx[0:rows] + cw_ref[1:2, sl] * vx[8:8 + rows]
                + cw_ref[2:3, sl] * vx[16:16 + rows] + cb_ref[:, sl])
        cb = _dot(h, win_ref[:, D + lo:D + lo + CONV_CHUNK])
        cg = _dot(h, win_ref[:, 3 * D + lo:3 * D + lo + CONV_CHUNK])
        mix[:, sl] = (cb * conv * jax.nn.silu(cg)).astype(BF16)

    n_keys = k_ref.shape[0]
    if not chained:
        row_seq = lax.broadcasted_iota(jnp.int32, (rows, n_keys), 0) % N_SEG
        key_seq = lax.broadcasted_iota(jnp.int32, (rows, n_keys), 1) // N_MEM
        own = row_seq == key_seq
    scale = HEAD_DIM ** -0.5

    def attn_head(hd):
        hs = slice(hd * HEAD_DIM, (hd + 1) * HEAD_DIM)
        q = _dot(hbf[...], win_ref[:, 6 * D + hd * HEAD_DIM:6 * D + (hd + 1) * HEAD_DIM])
        sc = lax.dot_general(q.astype(BF16), k_ref[:, hs], (((1,), (1,)), ((), ())),
                             preferred_element_type=F32) * scale
        if not chained:
            sc = jnp.where(own, sc, MASK_VALUE)
        e = jnp.exp(sc - jnp.max(sc, axis=-1, keepdims=True))
        p = e / jnp.sum(e, axis=-1, keepdims=True)
        o = _dot(p.astype(BF16), v_ref[:, hs])
        qg = _dot(hbf[...], win_ref[:, 7 * D + hd * HEAD_DIM:7 * D + (hd + 1) * HEAD_DIM])
        mix[:, 2 * D + hd * HEAD_DIM:2 * D + (hd + 1) * HEAD_DIM] = (o * jax.nn.silu(qg)).astype(BF16)

    def ssm_gate_proj():
        sgs[...] = _dot(hbf[...], win_ref[:, 5 * D:6 * D])

    def out_proj_conv_attn():
        oacc[...] = (_dot(mix[:, 0:D], wout_ref[0:D, :])
                     + _dot(mix[:, 2 * D:3 * D], wout_ref[2 * D:3 * D, :]))

    between = {0: lambda: conv_chunk(0), 1: lambda: conv_chunk(1), 2: ssm_gate_proj,
               3: lambda: attn_head(0), 4: lambda: attn_head(1), 5: lambda: attn_head(2),
               6: lambda: attn_head(3), 7: out_proj_conv_attn}

    su32[...] = _dot(hbf[...], win_ref[:, 4 * D:5 * D])
    for s in range(N_SLAB):
        cs = slice(s * SLAB, (s + 1) * SLAB)
        bu, hb = bus.at[s % 2], hbs.at[s % 2]
        bu[...] = _dot(su32[:, cs].astype(BF16), bblk_ref[s])
        between[s]()
        lr = jnp.broadcast_to(tab_ref[s, 0:1, re], blk)
        li = jnp.broadcast_to(tab_ref[s, 0:1, im], blk)

        def recurrence(hr, hi, emit):
            for i2 in range(ls // 2):
                blocks = []
                for i in (2 * i2, 2 * i2 + 1):
                    rw = slice(8 * i, 8 * i + 8)
                    hr, hi = lr * hr - li * hi + bu[rw, re], lr * hi + li * hr + bu[rw, im]
                    blocks.append(jnp.concatenate([hr, hi], axis=1))
                if emit:
                    hb[16 * i2:16 * i2 + 16, :] = jnp.concatenate(blocks, axis=0).astype(BF16)
            return hr, hi

        if chained:
            zero = jnp.zeros(blk, F32)
            tr, ti = recurrence(zero, zero, False)
            c0r = carry[s, 7:8, re]
            c0i = carry[s, 7:8, im]
            for sh in (1, 2, 4):
                keep = sub_s >= sh
                qr = jnp.where(keep, tab_ref[s, sh:sh + 1, re], 0.0)
                qi = jnp.where(keep, tab_ref[s, sh:sh + 1, im], 0.0)
                dr, di = _cmul(qr, qi, pltpu.roll(tr, sh, 0), pltpu.roll(ti, sh, 0))
                tr, ti = tr + dr, ti + di
            dr, di = _cmul(tab_ref[s, 1:1 + N_SEG, re], tab_ref[s, 1:1 + N_SEG, im], c0r, c0i)
            tr, ti = tr + dr, ti + di
            carry[s, :, re] = tr
            carry[s, :, im] = ti
            ssmout_ref[s, :, re] = tr
            ssmout_ref[s, :, im] = ti
            recurrence(jnp.where(sub_s == 0, c0r, pltpu.roll(tr, 1, 0)),
                       jnp.where(sub_s == 0, c0i, pltpu.roll(ti, 1, 0)), True)
        else:
            er, ei = recurrence(h0_ref[s, :, re], h0_ref[s, :, im], True)
            ssmout_ref[s, :, re] = er
            ssmout_ref[s, :, im] = ei

        y = _dot(hb[...], cblk_ref[s]) + dssm_ref[:, cs] * su32[:, cs]
        ys = jax.nn.gelu(y)
        ys32[:, cs] = ys
        ysb[:, cs] = ys.astype(BF16)

    gate = _dot(ysb[...], gw_ref[...]) + gb_ref[...]
    mix[:, D:2 * D] = (ys32[...] * jax.nn.sigmoid(gate) * jax.nn.silu(sgs[...])).astype(BF16)

    y = xbuf[slot].reshape(rows, D) + (oacc[...] + _dot(mix[:, D:2 * D], wout_ref[D:2 * D, :]))
    if final_norm:
        y = _rms(y, fg_ref[...])

    @pl.when(t >= 2)
    def _():
        for cp in _tile_copies(o_hbm, obuf, sem_out, t - 2, slot, True):
            cp.wait()

    obuf[slot] = y.reshape(ls, N_SEG, D)
    for cp in _tile_copies(o_hbm, obuf, sem_out, t, slot, True):
        cp.start()

    @pl.when(t == nt - 1)
    def _():
        @pl.when(t >= 1)
        def _():
            for cp in _tile_copies(o_hbm, obuf, sem_out, t - 1, 1 - slot, True):
                cp.wait()
        for cp in _tile_copies(o_hbm, obuf, sem_out, t, slot, True):
            cp.wait()


def _vmem_spec():
    return pl.BlockSpec(memory_space=pltpu.VMEM)


def _layer(x4, lw, tables, k, v, final_g, *, chained, final_norm, cinit=None, h0=None, name):
    nt, _, ls, _ = x4.shape
    rows = N_SEG * ls
    bblk, cblk, tab = tables
    in_hbm = [x4, lw['w_in'], lw['glu_w'], lw['w_out'], bblk, cblk, k, v]
    in_vmem = [lw['norm_g'], lw['conv_w'], lw['conv_b'], tab, lw['ssm_d'], lw['glu_b'], final_g]
    if not chained:
        in_vmem += [cinit, h0]
    args = in_hbm + in_vmem
    in_specs = [pl.BlockSpec(memory_space=pl.ANY) for _ in in_hbm] + [_vmem_spec() for _ in in_vmem]
    scratch = [
        pltpu.VMEM((2, ls, N_SEG, D), F32),
        pltpu.VMEM((2, ls, N_SEG, D), F32),
        pltpu.SemaphoreType.DMA((2, N_SEG)),
        pltpu.SemaphoreType.DMA((2, N_SEG)),
    ] + [pltpu.VMEM(a.shape, a.dtype) for a in in_hbm[1:]] + [
        pltpu.SemaphoreType.DMA((len(in_hbm) - 1,)),
        pltpu.VMEM((rows, D), BF16),
        pltpu.VMEM((D // CONV_CHUNK, rows + 16, CONV_CHUNK), F32),
        pltpu.VMEM((2, rows, D), F32),
        pltpu.VMEM((2, rows, D), BF16),
        pltpu.VMEM((rows, D), F32),
        pltpu.VMEM((rows, D), F32),
        pltpu.VMEM((rows, D), BF16),
        pltpu.VMEM((rows, 3 * D), BF16),
        pltpu.VMEM((rows, D), F32),
        pltpu.VMEM((rows, D), F32),
    ]
    if chained:
        scratch += [pltpu.VMEM((16, D), F32), pltpu.VMEM((N_SLAB, N_SEG, D), F32)]
    return pl.pallas_call(
        functools.partial(_layer_kernel, ls=ls, chained=chained, final_norm=final_norm),
        grid=(nt,),
        in_specs=in_specs,
        out_specs=(pl.BlockSpec(memory_space=pl.ANY),
                   pl.BlockSpec((2, N_SEG, D), lambda t: (0, 0, 0)),
                   pl.BlockSpec((N_SLAB, N_SEG, D), lambda t: (0, 0, 0))),
        out_shape=(jax.ShapeDtypeStruct(x4.shape, F32),
                   jax.ShapeDtypeStruct((2, N_SEG, D), F32),
                   jax.ShapeDtypeStruct((N_SLAB, N_SEG, D), F32)),
        scratch_shapes=scratch,
        compiler_params=pltpu.CompilerParams(dimension_semantics=("arbitrary",),
                                             vmem_limit_bytes=VMEM_LIMIT),
        name=name,
    )(*args)


def _state_from_slabs(st):
    depth, _, n, _ = st.shape
    st = jnp.transpose(st, (0, 2, 1, 3))
    return (st[..., :SLAB_STATES].reshape(depth, n, SSM_GROUPS, SSM_STATE),
            st[..., SLAB_STATES:].reshape(depth, n, SSM_GROUPS, SSM_STATE))


def kernel(x_prompt, x_sample, mem_prompt, cache_conv, state_ssm_re, state_ssm_im, cache_mem_k, cache_mem_v, norm_g, w_in, conv_w, conv_b, ssm_lambda_re, ssm_lambda_im, ssm_log_dt, ssm_b_re, ssm_b_im, ssm_c_re, ssm_c_im, ssm_d, ssm_glu_w, ssm_glu_b, mem_norm_g, w_kv, w_out, final_norm_g):
    depth = w_in.shape[0]
    bp, seq, _ = x_prompt.shape
    bs, seq_s, _ = x_sample.shape
    assert bp == 1 and bs == N_SEG
    ls_p = 32
    assert seq % (N_SEG * ls_p) == 0 and seq_s % 8 == 0
    xp = x_prompt.reshape(seq // (N_SEG * ls_p), N_SEG, ls_p, D)
    xs = x_sample.reshape(1, N_SEG, seq_s, D)
    fg = final_norm_g.reshape(1, D)
    w_in_b, glu_b16, w_out_b = w_in.astype(BF16), ssm_glu_w.astype(BF16), w_out.astype(BF16)
    ks_b = cache_mem_k.reshape(depth, bs * N_MEM, D).astype(BF16)
    vs_b = cache_mem_v.reshape(depth, bs * N_MEM, D).astype(BF16)
    h0 = jnp.concatenate([jnp.transpose(_slab_cols(state_ssm_re), (0, 2, 1, 3)),
                          jnp.transpose(_slab_cols(state_ssm_im), (0, 2, 1, 3))], axis=-1)
    cinit = jnp.transpose(cache_conv, (0, 2, 1, 3))
    p_cv, p_st, p_k, p_v, s_cv, s_st = ([] for _ in range(6))
    for l in range(depth):
        last = l == depth - 1
        lw = dict(norm_g=norm_g[l].reshape(1, D), w_in=w_in_b[l], conv_w=conv_w[l],
                  conv_b=conv_b[l].reshape(1, D), ssm_d=ssm_d[l].reshape(1, D),
                  glu_w=glu_b16[l], glu_b=ssm_glu_b[l].reshape(1, D), w_out=w_out_b[l])
        pw_re, pw_im, bb_re, bb_im = _discretize(ssm_lambda_re[l], ssm_lambda_im[l], ssm_log_dt[l],
                                                 ssm_b_re[l], ssm_b_im[l], ls_p)
        tables = _ssm_tables(pw_re, pw_im, bb_re, bb_im, ssm_c_re[l], ssm_c_im[l])
        kp, vp = _mem_kv(mem_prompt[0], mem_norm_g[l], w_kv[l])
        xp, cv, st = _layer(xp, lw, tables, kp.astype(BF16), vp.astype(BF16), fg,
                            chained=True, final_norm=last, name=f"layer{l}_prompt")
        p_cv.append(cv)
        p_st.append(st)
        p_k.append(kp)
        p_v.append(vp)
        xs, cv, st = _layer(xs, lw, tables, ks_b[l], vs_b[l], fg, chained=False, final_norm=last,
                            cinit=cinit[l], h0=h0[l], name=f"layer{l}_sample")
        s_cv.append(cv)
        s_st.append(st)
    p_cv, p_st, s_cv, s_st = (jnp.stack(a) for a in (p_cv, p_st, s_cv, s_st))
    p_re, p_im = _state_from_slabs(p_st[:, :, N_SEG - 1:, :])
    s_re, s_im = _state_from_slabs(s_st)
    kv_shape = (depth, 1, N_MEM, HEADS, HEAD_DIM)
    return (xp.reshape(bp, seq, D), xs.reshape(bs, seq_s, D),
            p_cv[:, :, N_SEG - 1, :][:, None], p_re, p_im,
            jnp.stack(p_k).reshape(kv_shape), jnp.stack(p_v).reshape(kv_shape),
            jnp.transpose(s_cv, (0, 2, 1, 3)), s_re, s_im)
```

```python
import functools

import jax
import jax.numpy as jnp
from jax import lax
from jax.experimental import pallas as pl
from jax.experimental.pallas import tpu as pltpu

F32 = jnp.float32
BF16 = jnp.bfloat16

D = 1024
N_SEG = 8
SSM_GROUP = 16
SSM_STATE = 64
SSM_GROUPS = 64
N_SLAB = 4
SLAB = 256
SLAB_GROUPS = 16
SLAB_STATES = 1024
HEADS = 4
HEAD_DIM = 256
N_MEM = 256
CONV_CHUNK = 512
EPS = 1e-6
MASK_VALUE = -1e30
VMEM_LIMIT = 60000 * 1024


def _dot(a, b):
    return jnp.dot(a, b, preferred_element_type=F32)


def _rms(x, g):
    ms = jnp.mean(x * x, axis=-1, keepdims=True)
    return x * lax.rsqrt(ms + EPS) * g


def _disc_kernel(lre_ref, lim_ref, ldt_ref, bre_ref, bim_ref,
                 pre_ref, pim_ref, bbre_ref, bbim_ref, *, ls):
    for l in range(lre_ref.shape[0]):
        lr = lre_ref[l]
        li = lim_ref[l]
        dt = jnp.exp(ldt_ref[l])
        ea = jnp.exp(lr * dt)
        ar = ea * jnp.cos(li * dt)
        ai = ea * jnp.sin(li * dt)
        nr = ar - 1.0
        den = lr * lr + li * li
        fr = (nr * lr + ai * li) / den
        fi = (ai * lr - nr * li) / den
        for i in range(SSM_GROUP):
            br = bre_ref[l, i]
            bi = bim_ref[l, i]
            bbre_ref[l, i] = fr * br - fi * bi
            bbim_ref[l, i] = fr * bi + fi * br
        pre_ref[l, 0] = ar
        pim_ref[l, 0] = ai
        sr, si = ar, ai
        for _ in range(ls - 1):
            sr, si = sr * ar - si * ai, sr * ai + si * ar
        cr, ci = sr, si
        for m in range(1, N_SEG + 1):
            pre_ref[l, m] = cr
            pim_ref[l, m] = ci
            cr, ci = cr * sr - ci * si, cr * si + ci * sr


def _discretize(lam_re, lam_im, log_dt, b_re, b_im, ls):
    depth, g, n = lam_re.shape
    bt_re = jnp.transpose(b_re, (0, 3, 1, 2))
    bt_im = jnp.transpose(b_im, (0, 3, 1, 2))
    npow = 1 + N_SEG
    return pl.pallas_call(
        functools.partial(_disc_kernel, ls=ls),
        out_shape=(jax.ShapeDtypeStruct((depth, npow, g, n), F32),
                   jax.ShapeDtypeStruct((depth, npow, g, n), F32),
                   jax.ShapeDtypeStruct((depth, SSM_GROUP, g, n), F32),
                   jax.ShapeDtypeStruct((depth, SSM_GROUP, g, n), F32)),
        name="ssm_discretize",
    )(lam_re, lam_im, log_dt.reshape(depth, g, 1), bt_re, bt_im)


def _slab_cols(a):
    return a.reshape(a.shape[:-2] + (N_SLAB, SLAB_STATES))


def _ssm_tables(p_re, p_im, bb_re, bb_im, c_re, c_im):
    depth = p_re.shape[0]

    def bblock(bb):
        rows = jnp.transpose(bb.reshape(depth, SSM_GROUP, N_SLAB, SLAB_GROUPS, SSM_STATE),
                             (0, 2, 3, 1, 4)).reshape(depth, N_SLAB, SLAB, 1, SSM_STATE)
        own = (jnp.arange(SLAB) // SSM_GROUP)[:, None, None] == jnp.arange(SLAB_GROUPS)[None, :, None]
        return jnp.where(own, rows, 0.0).reshape(depth, N_SLAB, SLAB, SLAB_STATES)

    def cblock(c):
        cols = jnp.transpose(c.reshape(depth, N_SLAB, SLAB_GROUPS, SSM_GROUP, SSM_STATE),
                             (0, 1, 2, 4, 3)).reshape(depth, N_SLAB, SLAB_STATES, 1, SSM_GROUP)
        own = (jnp.arange(SLAB_STATES) // SSM_STATE)[:, None, None] == jnp.arange(SLAB_GROUPS)[None, :, None]
        return jnp.where(own, cols, 0.0).reshape(depth, N_SLAB, SLAB_STATES, SLAB)

    bblk = jnp.concatenate([bblock(bb_re), bblock(bb_im)], axis=-1).astype(BF16)
    cblk = jnp.concatenate([cblock(c_re), -cblock(c_im)], axis=2).astype(BF16)
    tab = jnp.transpose(jnp.concatenate([_slab_cols(p_re), _slab_cols(p_im)], axis=-1), (0, 2, 1, 3))
    return bblk, cblk, tab


def _kv_kernel(mem_ref, g_ref, w_ref, k_ref, v_ref, kb_ref, vb_ref):
    x = mem_ref[...]
    for l in range(g_ref.shape[0]):
        h = _rms(x, g_ref[l:l + 1, :]).astype(BF16)
        kv = _dot(h, w_ref[l].astype(BF16))
        k_ref[l] = kv[:, :D]
        v_ref[l] = kv[:, D:]
        kb_ref[l] = kv[:, :D].astype(BF16)
        vb_ref[l] = kv[:, D:].astype(BF16)


def _mem_kv(mem, g, w_kv):
    depth = g.shape[0]
    n = mem.shape[0]
    f32 = jax.ShapeDtypeStruct((depth, n, D), F32)
    b16 = jax.ShapeDtypeStruct((depth, n, D), BF16)
    return pl.pallas_call(
        _kv_kernel,
        out_shape=(f32, f32, b16, b16),
        compiler_params=pltpu.CompilerParams(vmem_limit_bytes=VMEM_LIMIT),
        name="mem_kv",
    )(mem, g, w_kv)


def _cmul(ar, ai, br, bi):
    return ar * br - ai * bi, ar * bi + ai * br


def _tile_copies(hbm, buf, sems, t, slot, to_hbm):
    cps = []
    for j in range(N_SEG):
        vm = buf.at[slot, :, j, :]
        hb = hbm.at[t, j]
        src, dst = (vm, hb) if to_hbm else (hb, vm)
        cps.append(pltpu.make_async_copy(src, dst, sems.at[slot, j]))
    return cps


def _layer_kernel(*refs, ls, layer, chained, final_norm):
    rows = N_SEG * ls
    it = iter(refs)
    x_hbm, win_hbm, gw_hbm, wout_hbm, bblk_hbm, cblk_hbm, k_hbm, v_hbm = (next(it) for _ in range(8))
    ng_ref, cw_ref, cb_ref, tab_ref, dssm_ref, gb_ref, fg_ref = (next(it) for _ in range(7))
    if not chained:
        cinit_ref, h0_ref = next(it), next(it)
    o_hbm, convout_ref, ssmout_ref = (next(it) for _ in range(3))
    xbuf, obuf, sem_in, sem_out = (next(it) for _ in range(4))
    win_ref, gw_ref, wout_ref, bblk_ref, cblk_ref, k_ref, v_ref, sem_w = (next(it) for _ in range(8))
    hbf, vext, bus, hbs, su32, ys32, ysb, mix, sgs, oacc, qs, qgs = (next(it) for _ in range(12))
    if chained:
        tail, carry = next(it), next(it)

    t = pl.program_id(0)
    nt = pl.num_programs(0)
    slot = t % 2
    resident = [pltpu.make_async_copy(src.at[layer], dst, sem_w.at[n]) for n, (src, dst) in enumerate(
        ((win_hbm, win_ref), (gw_hbm, gw_ref), (wout_hbm, wout_ref), (bblk_hbm, bblk_ref),
         (cblk_hbm, cblk_ref), (k_hbm, k_ref), (v_hbm, v_ref)))]
    ng_ref, cw_ref, cb_ref, tab_ref, dssm_ref, gb_ref = (
        r.at[layer] for r in (ng_ref, cw_ref, cb_ref, tab_ref, dssm_ref, gb_ref))
    if not chained:
        cinit_ref, h0_ref = cinit_ref.at[layer], h0_ref.at[layer]

    @pl.when(t == 0)
    def _():
        for cp in resident:
            cp.start()
        for cp in _tile_copies(x_hbm, xbuf, sem_in, 0, 0, False):
            cp.start()
        if chained:
            tail[...] = jnp.zeros_like(tail)
            carry[...] = jnp.zeros_like(carry)

    @pl.when(t + 1 < nt)
    def _():
        for cp in _tile_copies(x_hbm, xbuf, sem_in, t + 1, 1 - slot, False):
            cp.start()

    for cp in _tile_copies(x_hbm, xbuf, sem_in, t, slot, False):
        cp.wait()

    @pl.when(t == 0)
    def _():
        for cp in resident:
            cp.wait()

    xv = xbuf[slot].reshape(rows, D)
    hbf[...] = _rms(xv, ng_ref[...]).astype(BF16)

    re = slice(0, SLAB_STATES)
    im = slice(SLAB_STATES, 2 * SLAB_STATES)
    blk = (N_SEG, SLAB_STATES)
    sub_c = lax.broadcasted_iota(jnp.int32, (N_SEG, CONV_CHUNK), 0)
    sub_s = lax.broadcasted_iota(jnp.int32, blk, 0)

    def conv_chunk(c):
        lo = c * CONV_CHUNK
        sl = slice(lo, lo + CONV_CHUNK)
        z = _dot(hbf[...], win_ref[:, 4 * lo:4 * (lo + CONV_CHUNK)])
        cx, cb, cc, cg = (z[:, n * CONV_CHUNK:(n + 1) * CONV_CHUNK] for n in range(4))
        v = cc * cx
        last2 = v[rows - 16:rows - 8]
        last1 = v[rows - 8:rows]
        if chained:
            b2 = pltpu.roll(jnp.where(sub_c == N_SEG - 1, tail[0:8, sl], last2), 1, 0)
            b1 = pltpu.roll(jnp.where(sub_c == N_SEG - 1, tail[8:16, sl], last1), 1, 0)
            tail[0:8, sl] = last2
            tail[8:16, sl] = last1
        else:
            b2 = cinit_ref[0, :, sl]
            b1 = cinit_ref[1, :, sl]
        convout_ref[0, :, sl] = last2
        convout_ref[1, :, sl] = last1
        vx = vext.at[c]
        vx[0:8] = b2
        vx[8:16] = b1
        vx[16:16 + rows] = v
        conv = (cw_ref[0:1, sl] * vx[0:rows] + cw_ref[1:2, sl] * vx[8:8 + rows]
                + cw_ref[2:3, sl] * vx[16:16 + rows] + cb_ref[:, sl])
        mix[:, sl] = (cb * conv * jax.nn.silu(cg)).astype(BF16)

    n_keys = k_ref.shape[0]
    if not chained:
        row_seq = lax.broadcasted_iota(jnp.int32, (rows, n_keys), 0) % N_SEG
        key_seq = lax.broadcasted_iota(jnp.int32, (rows, n_keys), 1) // N_MEM
        own = row_seq == key_seq
    scale = HEAD_DIM ** -0.5

    def attn_head(hd):
        hs = slice(hd * HEAD_DIM, (hd + 1) * HEAD_DIM)
        sc = lax.dot_general(qs[:, hs].astype(BF16), k_ref[:, hs], (((1,), (1,)), ((), ())),
                             preferred_element_type=F32) * scale
        if not chained:
            sc = jnp.where(own, sc, MASK_VALUE)
        e = jnp.exp(sc - jnp.max(sc, axis=-1, keepdims=True))
        p = e / jnp.sum(e, axis=-1, keepdims=True)
        o = _dot(p.astype(BF16), v_ref[:, hs])
        mix[:, D + hd * HEAD_DIM:D + (hd + 1) * HEAD_DIM] = (o * jax.nn.silu(qgs[:, hs])).astype(BF16)

    def attn_proj():
        z = _dot(hbf[...], win_ref[:, 6 * D:8 * D])
        qs[...] = z[:, :D]
        qgs[...] = z[:, D:]

    def out_proj_conv_attn():
        oacc[...] = _dot(mix[:, 0:2 * D], wout_ref[0:2 * D, :])

    between = {0: [lambda: conv_chunk(0)], 1: [lambda: conv_chunk(1)],
               2: [attn_proj, lambda: attn_head(0), lambda: attn_head(1)],
               3: [lambda: attn_head(2), lambda: attn_head(3), out_proj_conv_attn]}

    z = _dot(hbf[...], win_ref[:, 4 * D:6 * D])
    su32[...] = z[:, :D]
    sgs[...] = z[:, D:]
    for s in range(N_SLAB):
        cs = slice(s * SLAB, (s + 1) * SLAB)
        bu, hb = bus.at[s % 2], hbs.at[s % 2]
        bu[...] = _dot(su32[:, cs].astype(BF16), bblk_ref[s])
        for work in between[s]:
            work()
        lr = jnp.broadcast_to(tab_ref[s, 0:1, re], blk)
        li = jnp.broadcast_to(tab_ref[s, 0:1, im], blk)

        def recurrence(hr, hi, emit):
            for i2 in range(ls // 2):
                blocks = []
                for i in (2 * i2, 2 * i2 + 1):
                    rw = slice(8 * i, 8 * i + 8)
                    hr, hi = lr * hr - li * hi + bu[rw, re], lr * hi + li * hr + bu[rw, im]
                    blocks.append(jnp.concatenate([hr, hi], axis=1))
                if emit:
                    hb[16 * i2:16 * i2 + 16, :] = jnp.concatenate(blocks, axis=0).astype(BF16)
            return hr, hi

        if chained:
            zero = jnp.zeros(blk, F32)
            tr, ti = recurrence(zero, zero, False)
            c0r = carry[s, 7:8, re]
            c0i = carry[s, 7:8, im]
            for sh in (1, 2, 4):
                keep = sub_s >= sh
                qr = jnp.where(keep, tab_ref[s, sh:sh + 1, re], 0.0)
                qi = jnp.where(keep, tab_ref[s, sh:sh + 1, im], 0.0)
                dr, di = _cmul(qr, qi, pltpu.roll(tr, sh, 0), pltpu.roll(ti, sh, 0))
                tr, ti = tr + dr, ti + di
            dr, di = _cmul(tab_ref[s, 1:1 + N_SEG, re], tab_ref[s, 1:1 + N_SEG, im], c0r, c0i)
            tr, ti = tr + dr, ti + di
            carry[s, :, re] = tr
            carry[s, :, im] = ti
            ssmout_ref[s, :, re] = tr
            ssmout_ref[s, :, im] = ti
            recurrence(jnp.where(sub_s == 0, c0r, pltpu.roll(tr, 1, 0)),
                       jnp.where(sub_s == 0, c0i, pltpu.roll(ti, 1, 0)), True)
        else:
            er, ei = recurrence(h0_ref[s, :, re], h0_ref[s, :, im], True)
            ssmout_ref[s, :, re] = er
            ssmout_ref[s, :, im] = ei

        y = _dot(hb[...], cblk_ref[s]) + dssm_ref[:, cs] * su32[:, cs]
        ys = jax.nn.gelu(y)
        ys32[:, cs] = ys
        ysb[:, cs] = ys.astype(BF16)

    gate = _dot(ysb[...], gw_ref[...]) + gb_ref[...]
    mix[:, 2 * D:3 * D] = (ys32[...] * jax.nn.sigmoid(gate) * jax.nn.silu(sgs[...])).astype(BF16)

    y = xbuf[slot].reshape(rows, D) + (oacc[...] + _dot(mix[:, 2 * D:3 * D], wout_ref[2 * D:3 * D, :]))
    if final_norm:
        y = _rms(y, fg_ref[...])

    @pl.when(t >= 2)
    def _():
        for cp in _tile_copies(o_hbm, obuf, sem_out, t - 2, slot, True):
            cp.wait()

    obuf[slot] = y.reshape(ls, N_SEG, D)
    for cp in _tile_copies(o_hbm, obuf, sem_out, t, slot, True):
        cp.start()

    @pl.when(t == nt - 1)
    def _():
        @pl.when(t >= 1)
        def _():
            for cp in _tile_copies(o_hbm, obuf, sem_out, t - 1, 1 - slot, True):
                cp.wait()
        for cp in _tile_copies(o_hbm, obuf, sem_out, t, slot, True):
            cp.wait()


def _vmem_spec():
    return pl.BlockSpec(memory_space=pltpu.VMEM)


def _layer(x4, big, small, *, layer, chained, final_norm, name):
    nt, _, ls, _ = x4.shape
    rows = N_SEG * ls
    in_hbm = [x4] + list(big)
    in_vmem = list(small)
    args = in_hbm + in_vmem
    in_specs = [pl.BlockSpec(memory_space=pl.ANY) for _ in in_hbm] + [_vmem_spec() for _ in in_vmem]
    scratch = [
        pltpu.VMEM((2, ls, N_SEG, D), F32),
        pltpu.VMEM((2, ls, N_SEG, D), F32),
        pltpu.SemaphoreType.DMA((2, N_SEG)),
        pltpu.SemaphoreType.DMA((2, N_SEG)),
    ] + [pltpu.VMEM(a.shape[1:], a.dtype) for a in big] + [
        pltpu.SemaphoreType.DMA((len(in_hbm) - 1,)),
        pltpu.VMEM((rows, D), BF16),
        pltpu.VMEM((D // CONV_CHUNK, rows + 16, CONV_CHUNK), F32),
        pltpu.VMEM((2, rows, 2 * SLAB_STATES), F32),
        pltpu.VMEM((2, rows, 2 * SLAB_STATES), BF16),
        pltpu.VMEM((rows, D), F32),
        pltpu.VMEM((rows, D), F32),
        pltpu.VMEM((rows, D), BF16),
        pltpu.VMEM((rows, 3 * D), BF16),
        pltpu.VMEM((rows, D), F32),
        pltpu.VMEM((rows, D), F32),
        pltpu.VMEM((rows, D), F32),
        pltpu.VMEM((rows, D), F32),
    ]
    if chained:
        scratch += [pltpu.VMEM((16, D), F32), pltpu.VMEM((N_SLAB, N_SEG, 2 * SLAB_STATES), F32)]
    return pl.pallas_call(
        functools.partial(_layer_kernel, ls=ls, layer=layer, chained=chained, final_norm=final_norm),
        grid=(nt,),
        in_specs=in_specs,
        out_specs=(pl.BlockSpec(memory_space=pl.ANY),
                   pl.BlockSpec((2, N_SEG, D), lambda t: (0, 0, 0)),
                   pl.BlockSpec((N_SLAB, N_SEG, 2 * SLAB_STATES), lambda t: (0, 0, 0))),
        out_shape=(jax.ShapeDtypeStruct(x4.shape, F32),
                   jax.ShapeDtypeStruct((2, N_SEG, D), F32),
                   jax.ShapeDtypeStruct((N_SLAB, N_SEG, 2 * SLAB_STATES), F32)),
        scratch_shapes=scratch,
        compiler_params=pltpu.CompilerParams(dimension_semantics=("arbitrary",),
                                             vmem_limit_bytes=VMEM_LIMIT),
        name=name,
    )(*args)


def _state_from_slabs(st):
    depth, _, n, _ = st.shape
    st = jnp.transpose(st, (0, 2, 1, 3))
    return (st[..., :SLAB_STATES].reshape(depth, n, SSM_GROUPS, SSM_STATE),
            st[..., SLAB_STATES:].reshape(depth, n, SSM_GROUPS, SSM_STATE))


def kernel(x_prompt, x_sample, mem_prompt, cache_conv, state_ssm_re, state_ssm_im, cache_mem_k, cache_mem_v, norm_g, w_in, conv_w, conv_b, ssm_lambda_re, ssm_lambda_im, ssm_log_dt, ssm_b_re, ssm_b_im, ssm_c_re, ssm_c_im, ssm_d, ssm_glu_w, ssm_glu_b, mem_norm_g, w_kv, w_out, final_norm_g):
    depth = w_in.shape[0]
    bp, seq, _ = x_prompt.shape
    bs, seq_s, _ = x_sample.shape
    assert bp == 1 and bs == N_SEG
    ls_p = 32
    assert seq % (N_SEG * ls_p) == 0 and seq_s % 8 == 0
    xp = x_prompt.reshape(seq // (N_SEG * ls_p), N_SEG, ls_p, D)
    xs = x_sample.reshape(1, N_SEG, seq_s, D)
    n_chunk = D // CONV_CHUNK
    conv_cols = jnp.transpose(w_in[:, :, :4 * D].reshape(depth, D, 4, n_chunk, CONV_CHUNK),
                              (0, 1, 3, 2, 4)).reshape(depth, D, 4 * D)
    w_in_b = jnp.concatenate([conv_cols, w_in[:, :, 4 * D:]], axis=-1).astype(BF16)
    w_out_b = jnp.concatenate([w_out[:, :D], w_out[:, 2 * D:], w_out[:, D:2 * D]], axis=1).astype(BF16)
    glu_b16 = ssm_glu_w.astype(BF16)
    pw_re, pw_im, bb_re, bb_im = _discretize(ssm_lambda_re, ssm_lambda_im, ssm_log_dt,
                                             ssm_b_re, ssm_b_im, ls_p)
    bblk, cblk, tab = _ssm_tables(pw_re, pw_im, bb_re, bb_im, ssm_c_re, ssm_c_im)
    kp, vp, kp_b, vp_b = _mem_kv(mem_prompt[0], mem_norm_g, w_kv)
    ks_b = cache_mem_k.reshape(depth, bs * N_MEM, D).astype(BF16)
    vs_b = cache_mem_v.reshape(depth, bs * N_MEM, D).astype(BF16)
    h0 = jnp.concatenate([jnp.transpose(_slab_cols(state_ssm_re), (0, 2, 1, 3)),
                          jnp.transpose(_slab_cols(state_ssm_im), (0, 2, 1, 3))], axis=-1)
    cinit = jnp.transpose(cache_conv, (0, 2, 1, 3))
    row = lambda a: a.reshape(depth, 1, D)
    small = [row(norm_g), conv_w, row(conv_b), tab, row(ssm_d), row(ssm_glu_b),
             final_norm_g.reshape(1, D)]
    weights = [w_in_b, glu_b16, w_out_b, bblk, cblk]
    p_cv, p_st, s_cv, s_st = ([] for _ in range(4))
    for l in range(depth):
        last = l == depth - 1
        xp, cv, st = _layer(xp, weights + [kp_b, vp_b], small, layer=l, chained=True,
                            final_norm=last, name=f"layer{l}_prompt")
        p_cv.append(cv)
        p_st.append(st)
        xs, cv, st = _layer(xs, weights + [ks_b, vs_b], small + [cinit, h0], layer=l, chained=False,
                            final_norm=last, name=f"layer{l}_sample")
        s_cv.append(cv)
        s_st.append(st)
    p_cv, p_st, s_cv, s_st = (jnp.stack(a) for a in (p_cv, p_st, s_cv, s_st))
    p_re, p_im = _state_from_slabs(p_st[:, :, N_SEG - 1:, :])
    s_re, s_im = _state_from_slabs(s_st)
    kv_shape = (depth, 1, N_MEM, HEADS, HEAD_DIM)
    return (xp.reshape(bp, seq, D), xs.reshape(bs, seq_s, D),
            p_cv[:, :, N_SEG - 1, :][:, None], p_re, p_im,
            kp.reshape(kv_shape), vp.reshape(kv_shape),
            jnp.transpose(s_cv, (0, 2, 1, 3)), s_re, s_im)
```

```python
import functools

import jax
import jax.numpy as jnp
from jax import lax
from jax.experimental import pallas as pl
from jax.experimental.pallas import tpu as pltpu

F32 = jnp.float32
BF16 = jnp.bfloat16

D = 1024
N_SEG = 8
SSM_GROUP = 16
SSM_STATE = 64
SSM_GROUPS = 64
N_SLAB = 4
SLAB = 256
SLAB_GROUPS = 16
SLAB_STATES = 1024
HEADS = 4
HEAD_DIM = 256
N_MEM = 256
CONV_CHUNK = 512
EPS = 1e-6
MASK_VALUE = -1e30
N_RESIDENT = 3 + 4 * (D // CONV_CHUNK) + 1 + 3 + 2
VMEM_LIMIT = 60000 * 1024


def _dot(a, b):
    return jnp.dot(a, b, preferred_element_type=F32)


def _rms(x, g):
    ms = jnp.mean(x * x, axis=-1, keepdims=True)
    return x * lax.rsqrt(ms + EPS) * g


def _disc_kernel(lre_ref, lim_ref, ldt_ref, bre_ref, bim_ref, cre_ref, cim_ref,
                 pre_ref, pim_ref, bblk_ref, cblk_ref, *, ls):
    i32 = jnp.int32
    spread_b = (lax.broadcasted_iota(i32, (SSM_STATE, SLAB_STATES), 1) % SSM_STATE
                == lax.broadcasted_iota(i32, (SSM_STATE, SLAB_STATES), 0)).astype(BF16)
    own_b = (lax.broadcasted_iota(i32, (SLAB, SLAB_STATES), 0) // SSM_GROUP
             == lax.broadcasted_iota(i32, (SLAB, SLAB_STATES), 1) // SSM_STATE)
    spread_c = (lax.broadcasted_iota(i32, (SSM_GROUP, SLAB), 1) % SSM_GROUP
                == lax.broadcasted_iota(i32, (SSM_GROUP, SLAB), 0)).astype(BF16)
    own_c = (lax.broadcasted_iota(i32, (SLAB_STATES, SLAB), 0) // SSM_STATE
             == lax.broadcasted_iota(i32, (SLAB_STATES, SLAB), 1) // SSM_GROUP)

    def block_diag(parts, spread, own):
        rows = jnp.concatenate(parts, axis=0).astype(BF16)
        return jnp.where(own, _dot(rows, spread), 0.0)

    for l in range(lre_ref.shape[0]):
        lr = lre_ref[l]
        li = lim_ref[l]
        dt = jnp.exp(ldt_ref[l])
        ea = jnp.exp(lr * dt)
        ar = ea * jnp.cos(li * dt)
        ai = ea * jnp.sin(li * dt)
        nr = ar - 1.0
        den = lr * lr + li * li
        fr = (nr * lr + ai * li) / den
        fi = (ai * lr - nr * li) / den
        for s in range(N_SLAB):
            groups = range(s * SLAB_GROUPS, (s + 1) * SLAB_GROUPS)
            bb_re = [fr[g:g + 1, :] * bre_ref[l, g] - fi[g:g + 1, :] * bim_ref[l, g] for g in groups]
            bb_im = [fr[g:g + 1, :] * bim_ref[l, g] + fi[g:g + 1, :] * bre_ref[l, g] for g in groups]
            bblk_ref[l, s, :, 0:SLAB_STATES] = block_diag(bb_re, spread_b, own_b).astype(BF16)
            bblk_ref[l, s, :, SLAB_STATES:] = block_diag(bb_im, spread_b, own_b).astype(BF16)
            cblk_ref[l, s, 0:SLAB_STATES, :] = block_diag(
                [cre_ref[l, g] for g in groups], spread_c, own_c).astype(BF16)
            cblk_ref[l, s, SLAB_STATES:, :] = (-block_diag(
                [cim_ref[l, g] for g in groups], spread_c, own_c)).astype(BF16)
        pre_ref[l, 0] = ar
        pim_ref[l, 0] = ai
        sr, si = ar, ai
        for _ in range(ls - 1):
            sr, si = sr * ar - si * ai, sr * ai + si * ar
        cr, ci = sr, si
        for m in range(1, N_SEG + 1):
            pre_ref[l, m] = cr
            pim_ref[l, m] = ci
            cr, ci = cr * sr - ci * si, cr * si + ci * sr


def _slab_cols(a):
    return a.reshape(a.shape[:-2] + (N_SLAB, SLAB_STATES))


def _discretize(lam_re, lam_im, log_dt, b_re, b_im, c_re, c_im, ls):
    depth, g, n = lam_re.shape
    swap = lambda a: jnp.transpose(a, (0, 1, 3, 2))
    npow = 1 + N_SEG
    p_re, p_im, bblk, cblk = pl.pallas_call(
        functools.partial(_disc_kernel, ls=ls),
        out_shape=(jax.ShapeDtypeStruct((depth, npow, g, n), F32),
                   jax.ShapeDtypeStruct((depth, npow, g, n), F32),
                   jax.ShapeDtypeStruct((depth, N_SLAB, SLAB, 2 * SLAB_STATES), BF16),
                   jax.ShapeDtypeStruct((depth, N_SLAB, 2 * SLAB_STATES, SLAB), BF16)),
        compiler_params=pltpu.CompilerParams(vmem_limit_bytes=VMEM_LIMIT),
        name="ssm_discretize",
    )(lam_re, lam_im, log_dt.reshape(depth, g, 1), swap(b_re), swap(b_im), swap(c_re), swap(c_im))
    tab = jnp.transpose(jnp.concatenate([_slab_cols(p_re), _slab_cols(p_im)], axis=-1), (0, 2, 1, 3))
    return bblk, cblk, tab


def _kv_kernel(mem_ref, g_ref, w_ref, k_ref, v_ref, kb_ref, vb_ref):
    x = mem_ref[...]
    for l in range(g_ref.shape[0]):
        h = _rms(x, g_ref[l:l + 1, :]).astype(BF16)
        kv = _dot(h, w_ref[l].astype(BF16))
        k_ref[l] = kv[:, :D]
        v_ref[l] = kv[:, D:]
        kb_ref[l] = kv[:, :D].astype(BF16)
        vb_ref[l] = kv[:, D:].astype(BF16)


def _mem_kv(mem, g, w_kv):
    depth = g.shape[0]
    n = mem.shape[0]
    f32 = jax.ShapeDtypeStruct((depth, n, D), F32)
    b16 = jax.ShapeDtypeStruct((depth, n, D), BF16)
    return pl.pallas_call(
        _kv_kernel,
        out_shape=(f32, f32, b16, b16),
        compiler_params=pltpu.CompilerParams(vmem_limit_bytes=VMEM_LIMIT),
        name="mem_kv",
    )(mem, g, w_kv)


def _cmul(ar, ai, br, bi):
    return ar * br - ai * bi, ar * bi + ai * br


def _tile_copies(hbm, buf, sems, t, slot, to_hbm):
    cps = []
    for j in range(N_SEG):
        vm = buf.at[slot, :, j, :]
        hb = hbm.at[t, j]
        src, dst = (vm, hb) if to_hbm else (hb, vm)
        cps.append(pltpu.make_async_copy(src, dst, sems.at[slot, j]))
    return cps


def _layer_kernel(*refs, ls, layer, chained, final_norm):
    rows = N_SEG * ls
    it = iter(refs)
    x_hbm, win_hbm, gw_hbm, wout_hbm, bblk_hbm, cblk_hbm, k_hbm, v_hbm = (next(it) for _ in range(8))
    ng_ref, cw_ref, cb_ref, tab_ref, dssm_ref, gb_ref, fg_ref = (next(it) for _ in range(7))
    if not chained:
        cinit_ref, h0_ref = next(it), next(it)
    o_hbm, convout_ref, ssmout_ref = (next(it) for _ in range(3))
    xbuf, obuf, sem_in, sem_out = (next(it) for _ in range(4))
    win_ref, gw_ref, wout_ref, bblk_ref, cblk_ref, k_ref, v_ref, sem_w = (next(it) for _ in range(8))
    hbf, vext, bus, hbs, su32, ys32, ysb, mix, sgs, oacc, qs, qgs = (next(it) for _ in range(12))
    if chained:
        tail, carry = next(it), next(it)

    t = pl.program_id(0)
    nt = pl.num_programs(0)
    slot = t % 2
    resident = [(gw_hbm.at[layer], gw_ref), (bblk_hbm.at[layer], bblk_ref), (cblk_hbm.at[layer], cblk_ref)]
    n_chunk = D // CONV_CHUNK
    for kind in range(4):
        for c in range(n_chunk):
            src_lo = kind * D + c * CONV_CHUNK
            dst_lo = (c * 4 + kind) * CONV_CHUNK
            resident.append((win_hbm.at[layer, :, src_lo:src_lo + CONV_CHUNK],
                             win_ref.at[:, dst_lo:dst_lo + CONV_CHUNK]))
    resident.append((win_hbm.at[layer, :, 4 * D:8 * D], win_ref.at[:, 4 * D:8 * D]))
    for src_blk, dst_blk in ((0, 0), (2, 1), (1, 2)):
        resident.append((wout_hbm.at[layer, src_blk * D:(src_blk + 1) * D, :],
                         wout_ref.at[dst_blk * D:(dst_blk + 1) * D, :]))
    resident += [(k_hbm.at[layer], k_ref), (v_hbm.at[layer], v_ref)]
    resident = [pltpu.make_async_copy(src, dst, sem_w.at[n]) for n, (src, dst) in enumerate(resident)]

    ng_ref, cw_ref, cb_ref, tab_ref, dssm_ref, gb_ref = (
        r.at[layer] for r in (ng_ref, cw_ref, cb_ref, tab_ref, dssm_ref, gb_ref))
    if not chained:
        cinit_ref, h0_ref = cinit_ref.at[layer], h0_ref.at[layer]

    @pl.when(t == 0)
    def _():
        for cp in resident:
            cp.start()
        for cp in _tile_copies(x_hbm, xbuf, sem_in, 0, 0, False):
            cp.start()
        if chained:
            tail[...] = jnp.zeros_like(tail)
            carry[...] = jnp.zeros_like(carry)

    @pl.when(t + 1 < nt)
    def _():
        for cp in _tile_copies(x_hbm, xbuf, sem_in, t + 1, 1 - slot, False):
            cp.start()

    for cp in _tile_copies(x_hbm, xbuf, sem_in, t, slot, False):
        cp.wait()

    @pl.when(t == 0)
    def _():
        for cp in resident:
            cp.wait()

    xv = xbuf[slot].reshape(rows, D)
    hbf[...] = _rms(xv, ng_ref[...]).astype(BF16)

    re = slice(0, SLAB_STATES)
    im = slice(SLAB_STATES, 2 * SLAB_STATES)
    blk = (N_SEG, SLAB_STATES)
    sub_c = lax.broadcasted_iota(jnp.int32, (N_SEG, CONV_CHUNK), 0)
    sub_s = lax.broadcasted_iota(jnp.int32, blk, 0)

    def conv_chunk(c):
        lo = c * CONV_CHUNK
        sl = slice(lo, lo + CONV_CHUNK)
        z = _dot(hbf[...], win_ref[:, 4 * lo:4 * (lo + CONV_CHUNK)])
        cx, cb, cc, cg = (z[:, n * CONV_CHUNK:(n + 1) * CONV_CHUNK] for n in range(4))
        v = cc * cx
        last2 = v[rows - 16:rows - 8]
        last1 = v[rows - 8:rows]
        if chained:
            b2 = pltpu.roll(jnp.where(sub_c == N_SEG - 1, tail[0:8, sl], last2), 1, 0)
            b1 = pltpu.roll(jnp.where(sub_c == N_SEG - 1, tail[8:16, sl], last1), 1, 0)
            tail[0:8, sl] = last2
            tail[8:16, sl] = last1
        else:
            b2 = cinit_ref[0, :, sl]
            b1 = cinit_ref[1, :, sl]
        convout_ref[0, :, sl] = last2
        convout_ref[1, :, sl] = last1
        vx = vext.at[c]
        vx[0:8] = b2
        vx[8:16] = b1
        vx[16:16 + rows] = v
        conv = (cw_ref[0:1, sl] * vx[0:rows] + cw_ref[1:2, sl] * vx[8:8 + rows]
                + cw_ref[2:3, sl] * vx[16:16 + rows] + cb_ref[:, sl])
        mix[:, sl] = (cb * conv * jax.nn.silu(cg)).astype(BF16)

    n_keys = k_ref.shape[0]
    if not chained:
        row_seq = lax.broadcasted_iota(jnp.int32, (rows, n_keys), 0) % N_SEG
        key_seq = lax.broadcasted_iota(jnp.int32, (rows, n_keys), 1) // N_MEM
        own = row_seq == key_seq
    scale = HEAD_DIM ** -0.5

    def attn_head(hd):
        hs = slice(hd * HEAD_DIM, (hd + 1) * HEAD_DIM)
        sc = lax.dot_general(qs[:, hs].astype(BF16), k_ref[:, hs], (((1,), (1,)), ((), ())),
                             preferred_element_type=F32) * scale
        if not chained:
            sc = jnp.where(own, sc, MASK_VALUE)
        e = jnp.exp(sc - jnp.max(sc, axis=-1, keepdims=True))
        p = e / jnp.sum(e, axis=-1, keepdims=True)
        o = _dot(p.astype(BF16), v_ref[:, hs])
        mix[:, D + hd * HEAD_DIM:D + (hd + 1) * HEAD_DIM] = (o * jax.nn.silu(qgs[:, hs])).astype(BF16)

    def attn_proj():
        z = _dot(hbf[...], win_ref[:, 6 * D:8 * D])
        qs[...] = z[:, :D]
        qgs[...] = z[:, D:]

    def out_proj_conv_attn():
        oacc[...] = _dot(mix[:, 0:2 * D], wout_ref[0:2 * D, :])

    between = {0: [lambda: conv_chunk(0)], 1: [lambda: conv_chunk(1)],
               2: [attn_proj, lambda: attn_head(0), lambda: attn_head(1)],
               3: [lambda: attn_head(2), lambda: attn_head(3), out_proj_conv_attn]}

    z = _dot(hbf[...], win_ref[:, 4 * D:6 * D])
    su32[...] = z[:, :D]
    sgs[...] = z[:, D:]
    for s in range(N_SLAB):
        cs = slice(s * SLAB, (s + 1) * SLAB)
        bu, hb = bus.at[s % 2], hbs.at[s % 2]
        bu[...] = _dot(su32[:, cs].astype(BF16), bblk_ref[s])
        for work in between[s]:
            work()
        lr = jnp.broadcast_to(tab_ref[s, 0:1, re], blk)
        li = jnp.broadcast_to(tab_ref[s, 0:1, im], blk)

        def recurrence(hr, hi, emit):
            for i2 in range(ls // 2):
                blocks = []
                for i in (2 * i2, 2 * i2 + 1):
                    rw = slice(8 * i, 8 * i + 8)
                    hr, hi = lr * hr - li * hi + bu[rw, re], lr * hi + li * hr + bu[rw, im]
                    blocks.append(jnp.concatenate([hr, hi], axis=1))
                if emit:
                    hb[16 * i2:16 * i2 + 16, :] = jnp.concatenate(blocks, axis=0).astype(BF16)
            return hr, hi

        if chained:
            zero = jnp.zeros(blk, F32)
            tr, ti = recurrence(zero, zero, False)
            c0r = carry[s, 7:8, re]
            c0i = carry[s, 7:8, im]
            for sh in (1, 2, 4):
                keep = sub_s >= sh
                qr = jnp.where(keep, tab_ref[s, sh:sh + 1, re], 0.0)
                qi = jnp.where(keep, tab_ref[s, sh:sh + 1, im], 0.0)
                dr, di = _cmul(qr, qi, pltpu.roll(tr, sh, 0), pltpu.roll(ti, sh, 0))
                tr, ti = tr + dr, ti + di
            dr, di = _cmul(tab_ref[s, 1:1 + N_SEG, re], tab_ref[s, 1:1 + N_SEG, im], c0r, c0i)
            tr, ti = tr + dr, ti + di
            carry[s, :, re] = tr
            carry[s, :, im] = ti
            ssmout_ref[s, :, re] = tr
            ssmout_ref[s, :, im] = ti
            recurrence(jnp.where(sub_s == 0, c0r, pltpu.roll(tr, 1, 0)),
                       jnp.where(sub_s == 0, c0i, pltpu.roll(ti, 1, 0)), True)
        else:
            er, ei = recurrence(h0_ref[s, :, re], h0_ref[s, :, im], True)
            ssmout_ref[s, :, re] = er
            ssmout_ref[s, :, im] = ei

        y = _dot(hb[...], cblk_ref[s]) + dssm_ref[:, cs] * su32[:, cs]
        ys = jax.nn.gelu(y)
        ys32[:, cs] = ys
        ysb[:, cs] = ys.astype(BF16)

    gate = _dot(ysb[...], gw_ref[...]) + gb_ref[...]
    mix[:, 2 * D:3 * D] = (ys32[...] * jax.nn.sigmoid(gate) * jax.nn.silu(sgs[...])).astype(BF16)

    y = xbuf[slot].reshape(rows, D) + (oacc[...] + _dot(mix[:, 2 * D:3 * D], wout_ref[2 * D:3 * D, :]))
    if final_norm:
        y = _rms(y, fg_ref[...])

    @pl.when(t >= 2)
    def _():
        for cp in _tile_copies(o_hbm, obuf, sem_out, t - 2, slot, True):
            cp.wait()

    obuf[slot] = y.reshape(ls, N_SEG, D)
    for cp in _tile_copies(o_hbm, obuf, sem_out, t, slot, True):
        cp.start()

    @pl.when(t == nt - 1)
    def _():
        @pl.when(t >= 1)
        def _():
            for cp in _tile_copies(o_hbm, obuf, sem_out, t - 1, 1 - slot, True):
                cp.wait()
        for cp in _tile_copies(o_hbm, obuf, sem_out, t, slot, True):
            cp.wait()


def _vmem_spec():
    return pl.BlockSpec(memory_space=pltpu.VMEM)


def _layer(x4, big, small, *, layer, chained, final_norm, name):
    nt, _, ls, _ = x4.shape
    rows = N_SEG * ls
    in_hbm = [x4] + list(big)
    in_vmem = list(small)
    args = in_hbm + in_vmem
    in_specs = [pl.BlockSpec(memory_space=pl.ANY) for _ in in_hbm] + [_vmem_spec() for _ in in_vmem]
    scratch = [
        pltpu.VMEM((min(nt, 2), ls, N_SEG, D), F32),
        pltpu.VMEM((min(nt, 2), ls, N_SEG, D), F32),
        pltpu.SemaphoreType.DMA((2, N_SEG)),
        pltpu.SemaphoreType.DMA((2, N_SEG)),
    ] + [pltpu.VMEM(a.shape[1:], a.dtype) for a in big] + [
        pltpu.SemaphoreType.DMA((N_RESIDENT,)),
        pltpu.VMEM((rows, D), BF16),
        pltpu.VMEM((D // CONV_CHUNK, rows + 16, CONV_CHUNK), F32),
        pltpu.VMEM((2, rows, 2 * SLAB_STATES), F32),
        pltpu.VMEM((2, rows, 2 * SLAB_STATES), BF16),
        pltpu.VMEM((rows, D), F32),
        pltpu.VMEM((rows, D), F32),
        pltpu.VMEM((rows, D), BF16),
        pltpu.VMEM((rows, 3 * D), BF16),
        pltpu.VMEM((rows, D), F32),
        pltpu.VMEM((rows, D), F32),
        pltpu.VMEM((rows, D), F32),
        pltpu.VMEM((rows, D), F32),
    ]
    if chained:
        scratch += [pltpu.VMEM((16, D), F32), pltpu.VMEM((N_SLAB, N_SEG, 2 * SLAB_STATES), F32)]
    return pl.pallas_call(
        functools.partial(_layer_kernel, ls=ls, layer=layer, chained=chained, final_norm=final_norm),
        grid=(nt,),
        in_specs=in_specs,
        out_specs=(pl.BlockSpec(memory_space=pl.ANY),
                   pl.BlockSpec((2, N_SEG, D), lambda t: (0, 0, 0)),
                   pl.BlockSpec((N_SLAB, N_SEG, 2 * SLAB_STATES), lambda t: (0, 0, 0))),
        out_shape=(jax.ShapeDtypeStruct(x4.shape, F32),
                   jax.ShapeDtypeStruct((2, N_SEG, D), F32),
                   jax.ShapeDtypeStruct((N_SLAB, N_SEG, 2 * SLAB_STATES), F32)),
        scratch_shapes=scratch,
        compiler_params=pltpu.CompilerParams(dimension_semantics=("arbitrary",),
                                             vmem_limit_bytes=VMEM_LIMIT),
        name=name,
    )(*args)


def _state_from_slabs(st):
    depth, _, n, _ = st.shape
    st = jnp.transpose(st, (0, 2, 1, 3))
    return (st[..., :SLAB_STATES].reshape(depth, n, SSM_GROUPS, SSM_STATE),
            st[..., SLAB_STATES:].reshape(depth, n, SSM_GROUPS, SSM_STATE))


def kernel(x_prompt, x_sample, mem_prompt, cache_conv, state_ssm_re, state_ssm_im, cache_mem_k, cache_mem_v, norm_g, w_in, conv_w, conv_b, ssm_lambda_re, ssm_lambda_im, ssm_log_dt, ssm_b_re, ssm_b_im, ssm_c_re, ssm_c_im, ssm_d, ssm_glu_w, ssm_glu_b, mem_norm_g, w_kv, w_out, final_norm_g):
    depth = w_in.shape[0]
    bp, seq, _ = x_prompt.shape
    bs, seq_s, _ = x_sample.shape
    assert bp == 1 and bs == N_SEG
    ls_p = 32
    assert seq % (N_SEG * ls_p) == 0 and seq_s % 8 == 0
    xp = x_prompt.reshape(seq // (N_SEG * ls_p), N_SEG, ls_p, D)
    xs = x_sample.reshape(1, N_SEG, seq_s, D)
    w_in_b, glu_b16, w_out_b = w_in.astype(BF16), ssm_glu_w.astype(BF16), w_out.astype(BF16)
    bblk, cblk, tab = _discretize(ssm_lambda_re, ssm_lambda_im, ssm_log_dt,
                                  ssm_b_re, ssm_b_im, ssm_c_re, ssm_c_im, ls_p)
    kp, vp, kp_b, vp_b = _mem_kv(mem_prompt[0], mem_norm_g, w_kv)
    ks_b = cache_mem_k.reshape(depth, bs * N_MEM, D).astype(BF16)
    vs_b = cache_mem_v.reshape(depth, bs * N_MEM, D).astype(BF16)
    h0 = jnp.concatenate([jnp.transpose(_slab_cols(state_ssm_re), (0, 2, 1, 3)),
                          jnp.transpose(_slab_cols(state_ssm_im), (0, 2, 1, 3))], axis=-1)
    cinit = jnp.transpose(cache_conv, (0, 2, 1, 3))
    row = lambda a: a.reshape(depth, 1, D)
    small = [row(norm_g), conv_w, row(conv_b), tab, row(ssm_d), row(ssm_glu_b),
             final_norm_g.reshape(1, D)]
    weights = [w_in_b, glu_b16, w_out_b, bblk, cblk]
    p_cv, p_st, s_cv, s_st = ([] for _ in range(4))
    for l in range(depth):
        last = l == depth - 1
        xp, cv, st = _layer(xp, weights + [kp_b, vp_b], small, layer=l, chained=True,
                            final_norm=last, name=f"layer{l}_prompt")
        p_cv.append(cv)
        p_st.append(st)
        xs, cv, st = _layer(xs, weights + [ks_b, vs_b], small + [cinit, h0], layer=l, chained=False,
                            final_norm=last, name=f"layer{l}_sample")
        s_cv.append(cv)
        s_st.append(st)
    p_cv, p_st, s_cv, s_st = (jnp.stack(a) for a in (p_cv, p_st, s_cv, s_st))
    p_re, p_im = _state_from_slabs(p_st[:, :, N_SEG - 1:, :])
    s_re, s_im = _state_from_slabs(s_st)
    kv_shape = (depth, 1, N_MEM, HEADS, HEAD_DIM)
    return (xp.reshape(bp, seq, D), xs.reshape(bs, seq_s, D),
            p_cv[:, :, N_SEG - 1, :][:, None], p_re, p_im,
            kp.reshape(kv_shape), vp.reshape(kv_shape),
            jnp.transpose(s_cv, (0, 2, 1, 3)), s_re, s_im)
```

```python
import functools

import jax
import jax.numpy as jnp
from jax import lax
from jax.experimental import pallas as pl
from jax.experimental.pallas import tpu as pltpu

F32 = jnp.float32
BF16 = jnp.bfloat16

D = 1024
N_SEG = 8
SSM_GROUP = 16
SSM_STATE = 64
SSM_GROUPS = 64
N_SLAB = 4
SLAB = 256
SLAB_GROUPS = 16
SLAB_STATES = 1024
HEADS = 4
HEAD_DIM = 256
N_MEM = 256
CONV_CHUNK = 512
EPS = 1e-6
MASK_VALUE = -1e30
N_RESIDENT = 3 + 4 * (D // CONV_CHUNK) + 1 + 3 + 2
VMEM_LIMIT = 60000 * 1024


def _dot(a, b):
    return jnp.dot(a, b, preferred_element_type=F32)


def _rms(x, g):
    ms = jnp.mean(x * x, axis=-1, keepdims=True)
    return x * lax.rsqrt(ms + EPS) * g


def _disc_kernel(lre_ref, lim_ref, ldt_ref, bre_ref, bim_ref, cre_ref, cim_ref,
                 pre_ref, pim_ref, bblk_ref, cblk_ref, *, ls):
    i32 = jnp.int32
    spread_b = (lax.broadcasted_iota(i32, (SSM_STATE, SLAB_STATES), 1) % SSM_STATE
                == lax.broadcasted_iota(i32, (SSM_STATE, SLAB_STATES), 0)).astype(BF16)
    own_b = (lax.broadcasted_iota(i32, (SLAB, SLAB_STATES), 0) // SSM_GROUP
             == lax.broadcasted_iota(i32, (SLAB, SLAB_STATES), 1) // SSM_STATE)
    spread_c = (lax.broadcasted_iota(i32, (SSM_GROUP, SLAB), 1) % SSM_GROUP
                == lax.broadcasted_iota(i32, (SSM_GROUP, SLAB), 0)).astype(BF16)
    own_c = (lax.broadcasted_iota(i32, (SLAB_STATES, SLAB), 0) // SSM_STATE
             == lax.broadcasted_iota(i32, (SLAB_STATES, SLAB), 1) // SSM_GROUP)

    def block_diag(parts, spread, own):
        rows = jnp.concatenate(parts, axis=0).astype(BF16)
        return jnp.where(own, _dot(rows, spread), 0.0)

    for l in range(lre_ref.shape[0]):
        lr = lre_ref[l]
        li = lim_ref[l]
        dt = jnp.exp(ldt_ref[l])
        ea = jnp.exp(lr * dt)
        ar = ea * jnp.cos(li * dt)
        ai = ea * jnp.sin(li * dt)
        nr = ar - 1.0
        den = lr * lr + li * li
        fr = (nr * lr + ai * li) / den
        fi = (ai * lr - nr * li) / den
        for s in range(N_SLAB):
            groups = range(s * SLAB_GROUPS, (s + 1) * SLAB_GROUPS)
            bb_re = [fr[g:g + 1, :] * bre_ref[l, g] - fi[g:g + 1, :] * bim_ref[l, g] for g in groups]
            bb_im = [fr[g:g + 1, :] * bim_ref[l, g] + fi[g:g + 1, :] * bre_ref[l, g] for g in groups]
            bblk_ref[l, s, :, 0:SLAB_STATES] = block_diag(bb_re, spread_b, own_b).astype(BF16)
            bblk_ref[l, s, :, SLAB_STATES:] = block_diag(bb_im, spread_b, own_b).astype(BF16)
            cblk_ref[l, s, 0:SLAB_STATES, :] = block_diag(
                [cre_ref[l, g] for g in groups], spread_c, own_c).astype(BF16)
            cblk_ref[l, s, SLAB_STATES:, :] = (-block_diag(
                [cim_ref[l, g] for g in groups], spread_c, own_c)).astype(BF16)
        pre_ref[l, 0] = ar
        pim_ref[l, 0] = ai
        sr, si = ar, ai
        for _ in range(ls - 1):
            sr, si = sr * ar - si * ai, sr * ai + si * ar
        cr, ci = sr, si
        for m in range(1, N_SEG + 1):
            pre_ref[l, m] = cr
            pim_ref[l, m] = ci
            cr, ci = cr * sr - ci * si, cr * si + ci * sr


def _slab_cols(a):
    return a.reshape(a.shape[:-2] + (N_SLAB, SLAB_STATES))


def _discretize(lam_re, lam_im, log_dt, b_re, b_im, c_re, c_im, ls):
    depth, g, n = lam_re.shape
    swap = lambda a: jnp.transpose(a, (0, 1, 3, 2))
    npow = 1 + N_SEG
    p_re, p_im, bblk, cblk = pl.pallas_call(
        functools.partial(_disc_kernel, ls=ls),
        out_shape=(jax.ShapeDtypeStruct((depth, npow, g, n), F32),
                   jax.ShapeDtypeStruct((depth, npow, g, n), F32),
                   jax.ShapeDtypeStruct((depth, N_SLAB, SLAB, 2 * SLAB_STATES), BF16),
                   jax.ShapeDtypeStruct((depth, N_SLAB, 2 * SLAB_STATES, SLAB), BF16)),
        compiler_params=pltpu.CompilerParams(vmem_limit_bytes=VMEM_LIMIT),
        name="ssm_discretize",
    )(lam_re, lam_im, log_dt.reshape(depth, g, 1), swap(b_re), swap(b_im), swap(c_re), swap(c_im))
    tab = jnp.transpose(jnp.concatenate([_slab_cols(p_re), _slab_cols(p_im)], axis=-1), (0, 2, 1, 3))
    return bblk, cblk, tab


def _kv_kernel(mem_ref, g_ref, w_ref, k_ref, v_ref, kb_ref, vb_ref):
    x = mem_ref[...]
    for l in range(g_ref.shape[0]):
        h = _rms(x, g_ref[l:l + 1, :]).astype(BF16)
        kv = _dot(h, w_ref[l].astype(BF16))
        k_ref[l] = kv[:, :D]
        v_ref[l] = kv[:, D:]
        kb_ref[l] = kv[:, :D].astype(BF16)
        vb_ref[l] = kv[:, D:].astype(BF16)


def _mem_kv(mem, g, w_kv):
    depth = g.shape[0]
    n = mem.shape[0]
    f32 = jax.ShapeDtypeStruct((depth, n, D), F32)
    b16 = jax.ShapeDtypeStruct((depth, n, D), BF16)
    return pl.pallas_call(
        _kv_kernel,
        out_shape=(f32, f32, b16, b16),
        compiler_params=pltpu.CompilerParams(vmem_limit_bytes=VMEM_LIMIT),
        name="mem_kv",
    )(mem, g, w_kv)


def _cache_copies(k_hbm, v_hbm, stage, sems, n, slot, batch):
    return [pltpu.make_async_copy(src.at[n // batch, n % batch, :, hd, :], stage.at[slot, i, hd],
                                  sems.at[slot, i, hd])
            for i, src in enumerate((k_hbm, v_hbm)) for hd in range(HEADS)]


def _cache_kernel(k_hbm, v_hbm, ko_ref, vo_ref, stage, sems, *, batch):
    n = pl.program_id(0)
    slot = n % 2

    @pl.when(n == 0)
    def _():
        for cp in _cache_copies(k_hbm, v_hbm, stage, sems, 0, 0, batch):
            cp.start()

    @pl.when(n + 1 < pl.num_programs(0))
    def _():
        for cp in _cache_copies(k_hbm, v_hbm, stage, sems, n + 1, 1 - slot, batch):
            cp.start()

    for cp in _cache_copies(k_hbm, v_hbm, stage, sems, n, slot, batch):
        cp.wait()
    for i, out in enumerate((ko_ref, vo_ref)):
        for hd in range(HEADS):
            out[0, :, hd * HEAD_DIM:(hd + 1) * HEAD_DIM] = stage[slot, i, hd].astype(BF16)


def _cache_kv(cache_k, cache_v):
    depth, batch, n_mem, heads, head_dim = cache_k.shape
    assert (n_mem, heads, head_dim) == (N_MEM, HEADS, HEAD_DIM)
    out = jax.ShapeDtypeStruct((depth, batch * n_mem, D), BF16)
    blk = pl.BlockSpec((1, n_mem, D), lambda n: (n // batch, n % batch, 0))
    return pl.pallas_call(
        functools.partial(_cache_kernel, batch=batch),
        grid=(depth * batch,),
        in_specs=[pl.BlockSpec(memory_space=pl.ANY), pl.BlockSpec(memory_space=pl.ANY)],
        out_specs=(blk, blk),
        out_shape=(out, out),
        scratch_shapes=[pltpu.VMEM((2, 2, HEADS, N_MEM, HEAD_DIM), F32),
                        pltpu.SemaphoreType.DMA((2, 2, HEADS))],
        compiler_params=pltpu.CompilerParams(dimension_semantics=("arbitrary",)),
        name="cache_kv",
    )(cache_k, cache_v)


def _cmul(ar, ai, br, bi):
    return ar * br - ai * bi, ar * bi + ai * br


def _tile_copies(hbm, buf, sems, t, slot, to_hbm):
    cps = []
    for j in range(N_SEG):
        vm = buf.at[slot, :, j, :]
        hb = hbm.at[t, j]
        src, dst = (vm, hb) if to_hbm else (hb, vm)
        cps.append(pltpu.make_async_copy(src, dst, sems.at[slot, j]))
    return cps


def _layer_kernel(*refs, ls, layer, chained, final_norm):
    rows = N_SEG * ls
    it = iter(refs)
    x_hbm, win_hbm, gw_hbm, wout_hbm, bblk_hbm, cblk_hbm, k_hbm, v_hbm = (next(it) for _ in range(8))
    ng_ref, cw_ref, cb_ref, tab_ref, dssm_ref, gb_ref, fg_ref = (next(it) for _ in range(7))
    if not chained:
        cinit_ref, h0_ref = next(it), next(it)
    o_hbm, convout_ref, ssmout_ref = (next(it) for _ in range(3))
    xbuf, obuf, sem_in, sem_out = (next(it) for _ in range(4))
    win_ref, gw_ref, wout_ref, bblk_ref, cblk_ref, k_ref, v_ref, sem_w = (next(it) for _ in range(8))
    hbf, vext, bus, hbs, su32, ys32, ysb, mix, sgs, oacc, qs, qgs = (next(it) for _ in range(12))
    if chained:
        tail, carry = next(it), next(it)

    t = pl.program_id(0)
    nt = pl.num_programs(0)
    slot = t % 2
    resident = [(gw_hbm.at[layer], gw_ref), (bblk_hbm.at[layer], bblk_ref), (cblk_hbm.at[layer], cblk_ref)]
    n_chunk = D // CONV_CHUNK
    for kind in range(4):
        for c in range(n_chunk):
            src_lo = kind * D + c * CONV_CHUNK
            dst_lo = (c * 4 + kind) * CONV_CHUNK
            resident.append((win_hbm.at[layer, :, src_lo:src_lo + CONV_CHUNK],
                             win_ref.at[:, dst_lo:dst_lo + CONV_CHUNK]))
    resident.append((win_hbm.at[layer, :, 4 * D:8 * D], win_ref.at[:, 4 * D:8 * D]))
    for src_blk, dst_blk in ((0, 0), (2, 1), (1, 2)):
        resident.append((wout_hbm.at[layer, src_blk * D:(src_blk + 1) * D, :],
                         wout_ref.at[dst_blk * D:(dst_blk + 1) * D, :]))
    resident += [(k_hbm.at[layer], k_ref), (v_hbm.at[layer], v_ref)]
    resident = [pltpu.make_async_copy(src, dst, sem_w.at[n]) for n, (src, dst) in enumerate(resident)]

    ng_ref, cw_ref, cb_ref, tab_ref, dssm_ref, gb_ref = (
        r.at[layer] for r in (ng_ref, cw_ref, cb_ref, tab_ref, dssm_ref, gb_ref))
    if not chained:
        cinit_ref, h0_ref = cinit_ref.at[layer], h0_ref.at[layer]

    @pl.when(t == 0)
    def _():
        for cp in resident:
            cp.start()
        for cp in _tile_copies(x_hbm, xbuf, sem_in, 0, 0, False):
            cp.start()
        if chained:
            tail[...] = jnp.zeros_like(tail)
            carry[...] = jnp.zeros_like(carry)

    @pl.when(t + 1 < nt)
    def _():
        for cp in _tile_copies(x_hbm, xbuf, sem_in, t + 1, 1 - slot, False):
            cp.start()

    for cp in _tile_copies(x_hbm, xbuf, sem_in, t, slot, False):
        cp.wait()

    @pl.when(t == 0)
    def _():
        for cp in resident:
            cp.wait()

    xv = xbuf[slot].reshape(rows, D)
    hbf[...] = _rms(xv, ng_ref[...]).astype(BF16)

    re = slice(0, SLAB_STATES)
    im = slice(SLAB_STATES, 2 * SLAB_STATES)
    blk = (N_SEG, SLAB_STATES)
    sub_c = lax.broadcasted_iota(jnp.int32, (N_SEG, CONV_CHUNK), 0)
    sub_s = lax.broadcasted_iota(jnp.int32, blk, 0)

    def conv_chunk(c):
        lo = c * CONV_CHUNK
        sl = slice(lo, lo + CONV_CHUNK)
        z = _dot(hbf[...], win_ref[:, 4 * lo:4 * (lo + CONV_CHUNK)])
        cx, cb, cc, cg = (z[:, n * CONV_CHUNK:(n + 1) * CONV_CHUNK] for n in range(4))
        v = cc * cx
        last2 = v[rows - 16:rows - 8]
        last1 = v[rows - 8:rows]
        if chained:
            b2 = pltpu.roll(jnp.where(sub_c == N_SEG - 1, tail[0:8, sl], last2), 1, 0)
            b1 = pltpu.roll(jnp.where(sub_c == N_SEG - 1, tail[8:16, sl], last1), 1, 0)
            tail[0:8, sl] = last2
            tail[8:16, sl] = last1
        else:
            b2 = cinit_ref[0, :, sl]
            b1 = cinit_ref[1, :, sl]
        convout_ref[0, :, sl] = last2
        convout_ref[1, :, sl] = last1
        vx = vext.at[c]
        vx[0:8] = b2
        vx[8:16] = b1
        vx[16:16 + rows] = v
        conv = (cw_ref[0:1, sl] * vx[0:rows] + cw_ref[1:2, sl] * vx[8:8 + rows]
                + cw_ref[2:3, sl] * vx[16:16 + rows] + cb_ref[:, sl])
        mix[:, sl] = (cb * conv * jax.nn.silu(cg)).astype(BF16)

    n_keys = k_ref.shape[0]
    if not chained:
        row_seq = lax.broadcasted_iota(jnp.int32, (rows, n_keys), 0) % N_SEG
        key_seq = lax.broadcasted_iota(jnp.int32, (rows, n_keys), 1) // N_MEM
        own = row_seq == key_seq
    scale = HEAD_DIM ** -0.5

    def attn_head(hd):
        hs = slice(hd * HEAD_DIM, (hd + 1) * HEAD_DIM)
        sc = lax.dot_general(qs[:, hs].astype(BF16), k_ref[:, hs], (((1,), (1,)), ((), ())),
                             preferred_element_type=F32) * scale
        if not chained:
            sc = jnp.where(own, sc, MASK_VALUE)
        e = jnp.exp(sc - jnp.max(sc, axis=-1, keepdims=True))
        p = e / jnp.sum(e, axis=-1, keepdims=True)
        o = _dot(p.astype(BF16), v_ref[:, hs])
        mix[:, D + hd * HEAD_DIM:D + (hd + 1) * HEAD_DIM] = (o * jax.nn.silu(qgs[:, hs])).astype(BF16)

    def attn_proj():
        z = _dot(hbf[...], win_ref[:, 6 * D:8 * D])
        qs[...] = z[:, :D]
        qgs[...] = z[:, D:]

    def out_proj_conv_attn():
        oacc[...] = _dot(mix[:, 0:2 * D], wout_ref[0:2 * D, :])

    between = {0: [lambda: conv_chunk(0)], 1: [lambda: conv_chunk(1)],
               2: [attn_proj, lambda: attn_head(0), lambda: attn_head(1)],
               3: [lambda: attn_head(2), lambda: attn_head(3), out_proj_conv_attn]}

    z = _dot(hbf[...], win_ref[:, 4 * D:6 * D])
    su32[...] = z[:, :D]
    sgs[...] = z[:, D:]
    for s in range(N_SLAB):
        cs = slice(s * SLAB, (s + 1) * SLAB)
        bu, hb = bus.at[s % 2], hbs.at[s % 2]
        bu[...] = _dot(su32[:, cs].astype(BF16), bblk_ref[s])
        for work in between[s]:
            work()
        lr = jnp.broadcast_to(tab_ref[s, 0:1, re], blk)
        li = jnp.broadcast_to(tab_ref[s, 0:1, im], blk)

        def recurrence(hr, hi, emit):
            for i2 in range(ls // 2):
                blocks = []
                for i in (2 * i2, 2 * i2 + 1):
                    rw = slice(8 * i, 8 * i + 8)
                    hr, hi = lr * hr - li * hi + bu[rw, re], lr * hi + li * hr + bu[rw, im]
                    blocks.append(jnp.concatenate([hr, hi], axis=1))
                if emit:
                    hb[16 * i2:16 * i2 + 16, :] = jnp.concatenate(blocks, axis=0).astype(BF16)
            return hr, hi

        if chained:
            zero = jnp.zeros(blk, F32)
            tr, ti = recurrence(zero, zero, False)
            c0r = carry[s, 7:8, re]
            c0i = carry[s, 7:8, im]
            for sh in (1, 2, 4):
                keep = sub_s >= sh
                qr = jnp.where(keep, tab_ref[s, sh:sh + 1, re], 0.0)
                qi = jnp.where(keep, tab_ref[s, sh:sh + 1, im], 0.0)
                dr, di = _cmul(qr, qi, pltpu.roll(tr, sh, 0), pltpu.roll(ti, sh, 0))
                tr, ti = tr + dr, ti + di
            dr, di = _cmul(tab_ref[s, 1:1 + N_SEG, re], tab_ref[s, 1:1 + N_SEG, im], c0r, c0i)
            tr, ti = tr + dr, ti + di
            carry[s, :, re] = tr
            carry[s, :, im] = ti
            ssmout_ref[s, :, re] = tr
            ssmout_ref[s, :, im] = ti
            recurrence(jnp.where(sub_s == 0, c0r, pltpu.roll(tr, 1, 0)),
                       jnp.where(sub_s == 0, c0i, pltpu.roll(ti, 1, 0)), True)
        else:
            er, ei = recurrence(h0_ref[s, :, re], h0_ref[s, :, im], True)
            ssmout_ref[s, :, re] = er
            ssmout_ref[s, :, im] = ei

        y = _dot(hb[...], cblk_ref[s]) + dssm_ref[:, cs] * su32[:, cs]
        ys = jax.nn.gelu(y)
        ys32[:, cs] = ys
        ysb[:, cs] = ys.astype(BF16)

    gate = _dot(ysb[...], gw_ref[...]) + gb_ref[...]
    mix[:, 2 * D:3 * D] = (ys32[...] * jax.nn.sigmoid(gate) * jax.nn.silu(sgs[...])).astype(BF16)

    y = xbuf[slot].reshape(rows, D) + (oacc[...] + _dot(mix[:, 2 * D:3 * D], wout_ref[2 * D:3 * D, :]))
    if final_norm:
        y = _rms(y, fg_ref[...])

    @pl.when(t >= 2)
    def _():
        for cp in _tile_copies(o_hbm, obuf, sem_out, t - 2, slot, True):
            cp.wait()

    obuf[slot] = y.reshape(ls, N_SEG, D)
    for cp in _tile_copies(o_hbm, obuf, sem_out, t, slot, True):
        cp.start()

    @pl.when(t == nt - 1)
    def _():
        @pl.when(t >= 1)
        def _():
            for cp in _tile_copies(o_hbm, obuf, sem_out, t - 1, 1 - slot, True):
                cp.wait()
        for cp in _tile_copies(o_hbm, obuf, sem_out, t, slot, True):
            cp.wait()


def _vmem_spec():
    return pl.BlockSpec(memory_space=pltpu.VMEM)


def _layer(x4, big, small, *, layer, chained, final_norm, name):
    nt, _, ls, _ = x4.shape
    rows = N_SEG * ls
    in_hbm = [x4] + list(big)
    in_vmem = list(small)
    args = in_hbm + in_vmem
    in_specs = [pl.BlockSpec(memory_space=pl.ANY) for _ in in_hbm] + [_vmem_spec() for _ in in_vmem]
    scratch = [
        pltpu.VMEM((min(nt, 2), ls, N_SEG, D), F32),
        pltpu.VMEM((min(nt, 2), ls, N_SEG, D), F32),
        pltpu.SemaphoreType.DMA((2, N_SEG)),
        pltpu.SemaphoreType.DMA((2, N_SEG)),
    ] + [pltpu.VMEM(a.shape[1:], a.dtype) for a in big] + [
        pltpu.SemaphoreType.DMA((N_RESIDENT,)),
        pltpu.VMEM((rows, D), BF16),
        pltpu.VMEM((D // CONV_CHUNK, rows + 16, CONV_CHUNK), F32),
        pltpu.VMEM((2, rows, 2 * SLAB_STATES), F32),
        pltpu.VMEM((2, rows, 2 * SLAB_STATES), BF16),
        pltpu.VMEM((rows, D), F32),
        pltpu.VMEM((rows, D), F32),
        pltpu.VMEM((rows, D), BF16),
        pltpu.VMEM((rows, 3 * D), BF16),
        pltpu.VMEM((rows, D), F32),
        pltpu.VMEM((rows, D), F32),
        pltpu.VMEM((rows, D), F32),
        pltpu.VMEM((rows, D), F32),
    ]
    if chained:
        scratch += [pltpu.VMEM((16, D), F32), pltpu.VMEM((N_SLAB, N_SEG, 2 * SLAB_STATES), F32)]
    return pl.pallas_call(
        functools.partial(_layer_kernel, ls=ls, layer=layer, chained=chained, final_norm=final_norm),
        grid=(nt,),
        in_specs=in_specs,
        out_specs=(pl.BlockSpec(memory_space=pl.ANY),
                   pl.BlockSpec((2, N_SEG, D), lambda t: (0, 0, 0)),
                   pl.BlockSpec((N_SLAB, N_SEG, 2 * SLAB_STATES), lambda t: (0, 0, 0))),
        out_shape=(jax.ShapeDtypeStruct(x4.shape, F32),
                   jax.ShapeDtypeStruct((2, N_SEG, D), F32),
                   jax.ShapeDtypeStruct((N_SLAB, N_SEG, 2 * SLAB_STATES), F32)),
        scratch_shapes=scratch,
        compiler_params=pltpu.CompilerParams(dimension_semantics=("arbitrary",),
                                             vmem_limit_bytes=VMEM_LIMIT),
        name=name,
    )(*args)


def _state_from_slabs(st):
    depth, _, n, _ = st.shape
    st = jnp.transpose(st, (0, 2, 1, 3))
    return (st[..., :SLAB_STATES].reshape(depth, n, SSM_GROUPS, SSM_STATE),
            st[..., SLAB_STATES:].reshape(depth, n, SSM_GROUPS, SSM_STATE))


def kernel(x_prompt, x_sample, mem_prompt, cache_conv, state_ssm_re, state_ssm_im, cache_mem_k, cache_mem_v, norm_g, w_in, conv_w, conv_b, ssm_lambda_re, ssm_lambda_im, ssm_log_dt, ssm_b_re, ssm_b_im, ssm_c_re, ssm_c_im, ssm_d, ssm_glu_w, ssm_glu_b, mem_norm_g, w_kv, w_out, final_norm_g):
    depth = w_in.shape[0]
    bp, seq, _ = x_prompt.shape
    bs, seq_s, _ = x_sample.shape
    assert bp == 1 and bs == N_SEG
    ls_p = 32
    assert seq % (N_SEG * ls_p) == 0 and seq_s % 8 == 0
    xp = x_prompt.reshape(seq // (N_SEG * ls_p), N_SEG, ls_p, D)
    xs = x_sample.reshape(1, N_SEG, seq_s, D)
    w_in_b, glu_b16, w_out_b = w_in.astype(BF16), ssm_glu_w.astype(BF16), w_out.astype(BF16)
    bblk, cblk, tab = _discretize(ssm_lambda_re, ssm_lambda_im, ssm_log_dt,
                                  ssm_b_re, ssm_b_im, ssm_c_re, ssm_c_im, ls_p)
    kp, vp, kp_b, vp_b = _mem_kv(mem_prompt[0], mem_norm_g, w_kv)
    ks_b, vs_b = _cache_kv(cache_mem_k, cache_mem_v)
    h0 = jnp.concatenate([jnp.transpose(_slab_cols(state_ssm_re), (0, 2, 1, 3)),
                          jnp.transpose(_slab_cols(state_ssm_im), (0, 2, 1, 3))], axis=-1)
    cinit = jnp.transpose(cache_conv, (0, 2, 1, 3))
    row = lambda a: a.reshape(depth, 1, D)
    small = [row(norm_g), conv_w, row(conv_b), tab, row(ssm_d), row(ssm_glu_b),
             final_norm_g.reshape(1, D)]
    weights = [w_in_b, glu_b16, w_out_b, bblk, cblk]
    p_cv, p_st, s_cv, s_st = ([] for _ in range(4))
    for l in range(depth):
        last = l == depth - 1
        xp, cv, st = _layer(xp, weights + [kp_b, vp_b], small, layer=l, chained=True,
                            final_norm=last, name=f"layer{l}_prompt")
        p_cv.append(cv)
        p_st.append(st)
        xs, cv, st = _layer(xs, weights + [ks_b, vs_b], small + [cinit, h0], layer=l, chained=False,
                            final_norm=last, name=f"layer{l}_sample")
        s_cv.append(cv)
        s_st.append(st)
    p_cv, p_st, s_cv, s_st = (jnp.stack(a) for a in (p_cv, p_st, s_cv, s_st))
    p_re, p_im = _state_from_slabs(p_st[:, :, N_SEG - 1:, :])
    s_re, s_im = _state_from_slabs(s_st)
    kv_shape = (depth, 1, N_MEM, HEADS, HEAD_DIM)
    return (xp.reshape(bp, seq, D), xs.reshape(bs, seq_s, D),
            p_cv[:, :, N_SEG - 1, :][:, None], p_re, p_im,
            kp.reshape(kv_shape), vp.reshape(kv_shape),
            jnp.transpose(s_cv, (0, 2, 1, 3)), s_re, s_im)
```

```python
import functools

import jax
import jax.numpy as jnp
from jax import lax
from jax.experimental import pallas as pl
from jax.experimental.pallas import tpu as pltpu

F32 = jnp.float32
BF16 = jnp.bfloat16

D = 1024
N_SEG = 8
SSM_GROUP = 16
SSM_STATE = 64
SSM_GROUPS = 64
N_SLAB = 4
SLAB = 256
SLAB_GROUPS = 16
SLAB_STATES = 1024
HEADS = 4
HEAD_DIM = 256
N_MEM = 256
CONV_CHUNK = 512
EPS = 1e-6
MASK_VALUE = -1e30
N_RESIDENT = 3 + 4 * (D // CONV_CHUNK) + 1 + 3 + 2
VMEM_LIMIT = 60000 * 1024


def _dot(a, b):
    return jnp.dot(a, b, preferred_element_type=F32)


def _rms(x, g):
    ms = jnp.mean(x * x, axis=-1, keepdims=True)
    return x * lax.rsqrt(ms + EPS) * g


def _disc_kernel(lre_ref, lim_ref, ldt_ref, bre_ref, bim_ref, cre_ref, cim_ref,
                 pre_ref, pim_ref, bblk_ref, cblk_ref, *, ls):
    i32 = jnp.int32
    spread_b = (lax.broadcasted_iota(i32, (SSM_STATE, SLAB_STATES), 1) % SSM_STATE
                == lax.broadcasted_iota(i32, (SSM_STATE, SLAB_STATES), 0)).astype(BF16)
    own_b = (lax.broadcasted_iota(i32, (SLAB, SLAB_STATES), 0) // SSM_GROUP
             == lax.broadcasted_iota(i32, (SLAB, SLAB_STATES), 1) // SSM_STATE)
    spread_c = (lax.broadcasted_iota(i32, (SSM_GROUP, SLAB), 1) % SSM_GROUP
                == lax.broadcasted_iota(i32, (SSM_GROUP, SLAB), 0)).astype(BF16)
    own_c = (lax.broadcasted_iota(i32, (SLAB_STATES, SLAB), 0) // SSM_STATE
             == lax.broadcasted_iota(i32, (SLAB_STATES, SLAB), 1) // SSM_GROUP)

    def block_diag(parts, spread, own):
        rows = jnp.concatenate(parts, axis=0).astype(BF16)
        return jnp.where(own, _dot(rows, spread), 0.0)

    for l in range(lre_ref.shape[0]):
        lr = lre_ref[l]
        li = lim_ref[l]
        dt = jnp.exp(ldt_ref[l])
        ea = jnp.exp(lr * dt)
        ar = ea * jnp.cos(li * dt)
        ai = ea * jnp.sin(li * dt)
        nr = ar - 1.0
        den = lr * lr + li * li
        fr = (nr * lr + ai * li) / den
        fi = (ai * lr - nr * li) / den
        for s in range(N_SLAB):
            groups = range(s * SLAB_GROUPS, (s + 1) * SLAB_GROUPS)
            bb_re = [fr[g:g + 1, :] * bre_ref[l, g] - fi[g:g + 1, :] * bim_ref[l, g] for g in groups]
            bb_im = [fr[g:g + 1, :] * bim_ref[l, g] + fi[g:g + 1, :] * bre_ref[l, g] for g in groups]
            bblk_ref[l, s, :, 0:SLAB_STATES] = block_diag(bb_re, spread_b, own_b).astype(BF16)
            bblk_ref[l, s, :, SLAB_STATES:] = block_diag(bb_im, spread_b, own_b).astype(BF16)
            cblk_ref[l, s, 0:SLAB_STATES, :] = block_diag(
                [cre_ref[l, g] for g in groups], spread_c, own_c).astype(BF16)
            cblk_ref[l, s, SLAB_STATES:, :] = (-block_diag(
                [cim_ref[l, g] for g in groups], spread_c, own_c)).astype(BF16)
        pre_ref[l, 0] = ar
        pim_ref[l, 0] = ai
        sr, si = ar, ai
        for _ in range(ls - 1):
            sr, si = sr * ar - si * ai, sr * ai + si * ar
        cr, ci = sr, si
        for m in range(1, N_SEG + 1):
            pre_ref[l, m] = cr
            pim_ref[l, m] = ci
            cr, ci = cr * sr - ci * si, cr * si + ci * sr


def _slab_cols(a):
    return a.reshape(a.shape[:-2] + (N_SLAB, SLAB_STATES))


def _discretize(lam_re, lam_im, log_dt, b_re, b_im, c_re, c_im, ls):
    depth, g, n = lam_re.shape
    swap = lambda a: jnp.transpose(a, (0, 1, 3, 2))
    npow = 1 + N_SEG
    p_re, p_im, bblk, cblk = pl.pallas_call(
        functools.partial(_disc_kernel, ls=ls),
        out_shape=(jax.ShapeDtypeStruct((depth, npow, g, n), F32),
                   jax.ShapeDtypeStruct((depth, npow, g, n), F32),
                   jax.ShapeDtypeStruct((depth, N_SLAB, SLAB, 2 * SLAB_STATES), BF16),
                   jax.ShapeDtypeStruct((depth, N_SLAB, 2 * SLAB_STATES, SLAB), BF16)),
        compiler_params=pltpu.CompilerParams(vmem_limit_bytes=VMEM_LIMIT),
        name="ssm_discretize",
    )(lam_re, lam_im, log_dt.reshape(depth, g, 1), swap(b_re), swap(b_im), swap(c_re), swap(c_im))
    tab = jnp.transpose(jnp.concatenate([_slab_cols(p_re), _slab_cols(p_im)], axis=-1), (0, 2, 1, 3))
    return bblk, cblk, tab


def _kv_kernel(mem_ref, g_ref, w_ref, k_ref, v_ref, kb_ref, vb_ref):
    x = mem_ref[...]
    for l in range(g_ref.shape[0]):
        h = _rms(x, g_ref[l:l + 1, :]).astype(BF16)
        kv = _dot(h, w_ref[l].astype(BF16))
        k_ref[l] = kv[:, :D]
        v_ref[l] = kv[:, D:]
        kb_ref[l] = kv[:, :D].astype(BF16)
        vb_ref[l] = kv[:, D:].astype(BF16)


def _mem_kv(mem, g, w_kv):
    depth = g.shape[0]
    n = mem.shape[0]
    f32 = jax.ShapeDtypeStruct((depth, n, D), F32)
    b16 = jax.ShapeDtypeStruct((depth, n, D), BF16)
    return pl.pallas_call(
        _kv_kernel,
        out_shape=(f32, f32, b16, b16),
        compiler_params=pltpu.CompilerParams(vmem_limit_bytes=VMEM_LIMIT),
        name="mem_kv",
    )(mem, g, w_kv)


def _cache_copies(k_hbm, v_hbm, stage, sems, n, slot, batch):
    return [pltpu.make_async_copy(src.at[n // batch, n % batch, :, hd, :], stage.at[slot, i, hd],
                                  sems.at[slot, i, hd])
            for i, src in enumerate((k_hbm, v_hbm)) for hd in range(HEADS)]


def _cache_kernel(k_hbm, v_hbm, ko_ref, vo_ref, stage, sems, *, batch):
    n = pl.program_id(0)
    slot = n % 2

    @pl.when(n == 0)
    def _():
        for cp in _cache_copies(k_hbm, v_hbm, stage, sems, 0, 0, batch):
            cp.start()

    @pl.when(n + 1 < pl.num_programs(0))
    def _():
        for cp in _cache_copies(k_hbm, v_hbm, stage, sems, n + 1, 1 - slot, batch):
            cp.start()

    for cp in _cache_copies(k_hbm, v_hbm, stage, sems, n, slot, batch):
        cp.wait()
    for i, out in enumerate((ko_ref, vo_ref)):
        for hd in range(HEADS):
            out[0, :, hd * HEAD_DIM:(hd + 1) * HEAD_DIM] = stage[slot, i, hd].astype(BF16)


def _cache_kv(cache_k, cache_v):
    depth, batch, n_mem, heads, head_dim = cache_k.shape
    assert (n_mem, heads, head_dim) == (N_MEM, HEADS, HEAD_DIM)
    out = jax.ShapeDtypeStruct((depth, batch * n_mem, D), BF16)
    blk = pl.BlockSpec((1, n_mem, D), lambda n: (n // batch, n % batch, 0))
    return pl.pallas_call(
        functools.partial(_cache_kernel, batch=batch),
        grid=(depth * batch,),
        in_specs=[pl.BlockSpec(memory_space=pl.ANY), pl.BlockSpec(memory_space=pl.ANY)],
        out_specs=(blk, blk),
        out_shape=(out, out),
        scratch_shapes=[pltpu.VMEM((2, 2, HEADS, N_MEM, HEAD_DIM), F32),
                        pltpu.SemaphoreType.DMA((2, 2, HEADS))],
        compiler_params=pltpu.CompilerParams(dimension_semantics=("arbitrary",)),
        name="cache_kv",
    )(cache_k, cache_v)


def _cmul(ar, ai, br, bi):
    return ar * br - ai * bi, ar * bi + ai * br


def _tile_copies(hbm, buf, sems, t, slot, to_hbm):
    cps = []
    for j in range(N_SEG):
        vm = buf.at[slot, :, j, :]
        hb = hbm.at[t, j]
        src, dst = (vm, hb) if to_hbm else (hb, vm)
        cps.append(pltpu.make_async_copy(src, dst, sems.at[slot, j]))
    return cps


def _layer_kernel(*refs, ls, layer, chained, final_norm):
    rows = N_SEG * ls
    it = iter(refs)
    x_hbm, win_hbm, gw_hbm, wout_hbm, bblk_hbm, cblk_hbm, k_hbm, v_hbm = (next(it) for _ in range(8))
    ng_ref, cw_ref, cb_ref, tab_ref, dssm_ref, gb_ref, fg_ref = (next(it) for _ in range(7))
    if not chained:
        cinit_ref, h0_ref = next(it), next(it)
    o_hbm, convout_ref, ssmout_ref = (next(it) for _ in range(3))
    xbuf, obuf, sem_in, sem_out = (next(it) for _ in range(4))
    win_ref, gw_ref, wout_ref, bblk_ref, cblk_ref, k_ref, v_ref, sem_w = (next(it) for _ in range(8))
    hbf, vext, bus, hbs, su32, ys32, ysb, mix, sgs, oacc, qs, qgs = (next(it) for _ in range(12))
    if chained:
        tail, carry = next(it), next(it)

    t = pl.program_id(0)
    nt = pl.num_programs(0)
    slot = t % 2
    resident = [(gw_hbm.at[layer], gw_ref), (bblk_hbm.at[layer], bblk_ref), (cblk_hbm.at[layer], cblk_ref)]
    n_chunk = D // CONV_CHUNK
    for kind in range(4):
        for c in range(n_chunk):
            src_lo = kind * D + c * CONV_CHUNK
            dst_lo = (c * 4 + kind) * CONV_CHUNK
            resident.append((win_hbm.at[layer, :, src_lo:src_lo + CONV_CHUNK],
                             win_ref.at[:, dst_lo:dst_lo + CONV_CHUNK]))
    resident.append((win_hbm.at[layer, :, 4 * D:8 * D], win_ref.at[:, 4 * D:8 * D]))
    for src_blk, dst_blk in ((0, 0), (2, 1), (1, 2)):
        resident.append((wout_hbm.at[layer, src_blk * D:(src_blk + 1) * D, :],
                         wout_ref.at[dst_blk * D:(dst_blk + 1) * D, :]))
    resident += [(k_hbm.at[layer], k_ref), (v_hbm.at[layer], v_ref)]
    resident = [pltpu.make_async_copy(src, dst, sem_w.at[n]) for n, (src, dst) in enumerate(resident)]

    ng_ref, cw_ref, cb_ref, tab_ref, dssm_ref, gb_ref = (
        r.at[layer] for r in (ng_ref, cw_ref, cb_ref, tab_ref, dssm_ref, gb_ref))
    if not chained:
        cinit_ref, h0_ref = cinit_ref.at[layer], h0_ref.at[layer]

    @pl.when(t == 0)
    def _():
        for cp in resident:
            cp.start()
        for cp in _tile_copies(x_hbm, xbuf, sem_in, 0, 0, False):
            cp.start()
        if chained:
            tail[...] = jnp.zeros_like(tail)
            carry[...] = jnp.zeros_like(carry)

    @pl.when(t + 1 < nt)
    def _():
        for cp in _tile_copies(x_hbm, xbuf, sem_in, t + 1, 1 - slot, False):
            cp.start()

    for cp in _tile_copies(x_hbm, xbuf, sem_in, t, slot, False):
        cp.wait()

    @pl.when(t == 0)
    def _():
        for cp in resident:
            cp.wait()

    xv = xbuf[slot].reshape(rows, D)
    hbf[...] = _rms(xv, ng_ref[...]).astype(BF16)

    re = slice(0, SLAB_STATES)
    im = slice(SLAB_STATES, 2 * SLAB_STATES)
    blk = (N_SEG, SLAB_STATES)
    sub_c = lax.broadcasted_iota(jnp.int32, (N_SEG, CONV_CHUNK), 0)
    sub_s = lax.broadcasted_iota(jnp.int32, blk, 0)

    def conv_chunk(c):
        lo = c * CONV_CHUNK
        sl = slice(lo, lo + CONV_CHUNK)
        z = _dot(hbf[...], win_ref[:, 4 * lo:4 * (lo + CONV_CHUNK)])
        cx, cb, cc, cg = (z[:, n * CONV_CHUNK:(n + 1) * CONV_CHUNK] for n in range(4))
        v = cc * cx
        last2 = v[rows - 16:rows - 8]
        last1 = v[rows - 8:rows]
        if chained:
            b2 = pltpu.roll(jnp.where(sub_c == N_SEG - 1, tail[0:8, sl], last2), 1, 0)
            b1 = pltpu.roll(jnp.where(sub_c == N_SEG - 1, tail[8:16, sl], last1), 1, 0)
            tail[0:8, sl] = last2
            tail[8:16, sl] = last1
        else:
            b2 = cinit_ref[0, :, sl]
            b1 = cinit_ref[1, :, sl]
        convout_ref[0, :, sl] = last2
        convout_ref[1, :, sl] = last1
        vx = vext.at[c]
        vx[0:8] = b2
        vx[8:16] = b1
        vx[16:16 + rows] = v
        conv = (cw_ref[0:1, sl] * vx[0:rows] + cw_ref[1:2, sl] * vx[8:8 + rows]
                + cw_ref[2:3, sl] * vx[16:16 + rows] + cb_ref[:, sl])
        mix[:, sl] = (cb * conv * jax.nn.silu(cg)).astype(BF16)

    n_keys = k_ref.shape[0]
    if not chained:
        row_seq = lax.broadcasted_iota(jnp.int32, (rows, n_keys), 0) % N_SEG
        key_seq = lax.broadcasted_iota(jnp.int32, (rows, n_keys), 1) // N_MEM
        own = row_seq == key_seq
    scale = HEAD_DIM ** -0.5

    def attn_head(hd):
        hs = slice(hd * HEAD_DIM, (hd + 1) * HEAD_DIM)
        sc = lax.dot_general(qs[:, hs].astype(BF16), k_ref[:, hs], (((1,), (1,)), ((), ())),
                             preferred_element_type=F32) * scale
        if not chained:
            sc = jnp.where(own, sc, MASK_VALUE)
        e = jnp.exp(sc - jnp.max(sc, axis=-1, keepdims=True))
        p = e / jnp.sum(e, axis=-1, keepdims=True)
        o = _dot(p.astype(BF16), v_ref[:, hs])
        mix[:, D + hd * HEAD_DIM:D + (hd + 1) * HEAD_DIM] = (o * jax.nn.silu(qgs[:, hs])).astype(BF16)

    def attn_proj():
        z = _dot(hbf[...], win_ref[:, 6 * D:8 * D])
        qs[...] = z[:, :D]
        qgs[...] = z[:, D:]

    def out_proj_conv_attn():
        oacc[...] = _dot(mix[:, 0:2 * D], wout_ref[0:2 * D, :])

    between = {0: [lambda: conv_chunk(0)], 1: [lambda: conv_chunk(1)],
               2: [attn_proj, lambda: attn_head(0), lambda: attn_head(1)],
               3: [lambda: attn_head(2), lambda: attn_head(3), out_proj_conv_attn]}

    z = _dot(hbf[...], win_ref[:, 4 * D:6 * D])
    su32[...] = z[:, :D]
    sgs[...] = z[:, D:]
    for s in range(N_SLAB):
        cs = slice(s * SLAB, (s + 1) * SLAB)
        bu, hb = bus.at[s % 2], hbs.at[s % 2]
        bu[...] = _dot(su32[:, cs].astype(BF16), bblk_ref[s])
        for work in between[s]:
            work()
        lr = jnp.broadcast_to(tab_ref[s, 0:1, re], blk)
        li = jnp.broadcast_to(tab_ref[s, 0:1, im], blk)

        def recurrence(hr, hi, emit):
            for i2 in range(ls // 2):
                blocks = []
                for i in (2 * i2, 2 * i2 + 1):
                    rw = slice(8 * i, 8 * i + 8)
                    hr, hi = lr * hr - li * hi + bu[rw, re], lr * hi + li * hr + bu[rw, im]
                    blocks.append(jnp.concatenate([hr, hi], axis=1))
                if emit:
                    hb[16 * i2:16 * i2 + 16, :] = jnp.concatenate(blocks, axis=0).astype(BF16)
            return hr, hi

        if chained:
            zero = jnp.zeros(blk, F32)
            tr, ti = recurrence(zero, zero, False)
            c0r = carry[s, 7:8, re]
            c0i = carry[s, 7:8, im]
            for sh in (1, 2, 4):
                keep = sub_s >= sh
                qr = jnp.where(keep, tab_ref[s, sh:sh + 1, re], 0.0)
                qi = jnp.where(keep, tab_ref[s, sh:sh + 1, im], 0.0)
                dr, di = _cmul(qr, qi, pltpu.roll(tr, sh, 0), pltpu.roll(ti, sh, 0))
                tr, ti = tr + dr, ti + di
            dr, di = _cmul(tab_ref[s, 1:1 + N_SEG, re], tab_ref[s, 1:1 + N_SEG, im], c0r, c0i)
            tr, ti = tr + dr, ti + di
            carry[s, :, re] = tr
            carry[s, :, im] = ti
            ssmout_ref[s, :, re] = tr
            ssmout_ref[s, :, im] = ti
            recurrence(jnp.where(sub_s == 0, c0r, pltpu.roll(tr, 1, 0)),
                       jnp.where(sub_s == 0, c0i, pltpu.roll(ti, 1, 0)), True)
        else:
            er, ei = recurrence(h0_ref[s, :, re], h0_ref[s, :, im], True)
            ssmout_ref[s, :, re] = er
            ssmout_ref[s, :, im] = ei

        y = _dot(hb[...], cblk_ref[s]) + dssm_ref[:, cs] * su32[:, cs]
        ys = jax.nn.gelu(y)
        ys32[:, cs] = ys
        ysb[:, cs] = ys.astype(BF16)

    gate = _dot(ysb[...], gw_ref[...]) + gb_ref[...]
    mix[:, 2 * D:3 * D] = (ys32[...] * jax.nn.sigmoid(gate) * jax.nn.silu(sgs[...])).astype(BF16)

    y = xbuf[slot].reshape(rows, D) + (oacc[...] + _dot(mix[:, 2 * D:3 * D], wout_ref[2 * D:3 * D, :]))
    if final_norm:
        y = _rms(y, fg_ref[...])

    @pl.when(t >= 2)
    def _():
        for cp in _tile_copies(o_hbm, obuf, sem_out, t - 2, slot, True):
            cp.wait()

    obuf[slot] = y.reshape(ls, N_SEG, D)
    for cp in _tile_copies(o_hbm, obuf, sem_out, t, slot, True):
        cp.start()

    @pl.when(t == nt - 1)
    def _():
        @pl.when(t >= 1)
        def _():
            for cp in _tile_copies(o_hbm, obuf, sem_out, t - 1, 1 - slot, True):
                cp.wait()
        for cp in _tile_copies(o_hbm, obuf, sem_out, t, slot, True):
            cp.wait()


def _vmem_spec():
    return pl.BlockSpec(memory_space=pltpu.VMEM)


def _layer(x4, big, small, *, layer, chained, final_norm, name):
    nt, _, ls, _ = x4.shape
    rows = N_SEG * ls
    in_hbm = [x4] + list(big)
    in_vmem = list(small)
    args = in_hbm + in_vmem
    in_specs = [pl.BlockSpec(memory_space=pl.ANY) for _ in in_hbm] + [_vmem_spec() for _ in in_vmem]
    scratch = [
        pltpu.VMEM((min(nt, 2), ls, N_SEG, D), F32),
        pltpu.VMEM((min(nt, 2), ls, N_SEG, D), F32),
        pltpu.SemaphoreType.DMA((2, N_SEG)),
        pltpu.SemaphoreType.DMA((2, N_SEG)),
    ] + [pltpu.VMEM(a.shape[1:], a.dtype) for a in big] + [
        pltpu.SemaphoreType.DMA((N_RESIDENT,)),
        pltpu.VMEM((rows, D), BF16),
        pltpu.VMEM((D // CONV_CHUNK, rows + 16, CONV_CHUNK), F32),
        pltpu.VMEM((2, rows, 2 * SLAB_STATES), F32),
        pltpu.VMEM((2, rows, 2 * SLAB_STATES), BF16),
        pltpu.VMEM((rows, D), F32),
        pltpu.VMEM((rows, D), F32),
        pltpu.VMEM((rows, D), BF16),
        pltpu.VMEM((rows, 3 * D), BF16),
        pltpu.VMEM((rows, D), F32),
        pltpu.VMEM((rows, D), F32),
        pltpu.VMEM((rows, D), F32),
        pltpu.VMEM((rows, D), F32),
    ]
    if chained:
        scratch += [pltpu.VMEM((16, D), F32), pltpu.VMEM((N_SLAB, N_SEG, 2 * SLAB_STATES), F32)]
    return pl.pallas_call(
        functools.partial(_layer_kernel, ls=ls, layer=layer, chained=chained, final_norm=final_norm),
        grid=(nt,),
        in_specs=in_specs,
        out_specs=(pl.BlockSpec(memory_space=pl.ANY),
                   pl.BlockSpec((2, N_SEG, D), lambda t: (0, 0, 0)),
                   pl.BlockSpec((N_SLAB, N_SEG, 2 * SLAB_STATES), lambda t: (0, 0, 0))),
        out_shape=(jax.ShapeDtypeStruct(x4.shape, F32),
                   jax.ShapeDtypeStruct((2, N_SEG, D), F32),
                   jax.ShapeDtypeStruct((N_SLAB, N_SEG, 2 * SLAB_STATES), F32)),
        scratch_shapes=scratch,
        compiler_params=pltpu.CompilerParams(dimension_semantics=("arbitrary",),
                                             vmem_limit_bytes=VMEM_LIMIT),
        name=name,
    )(*args)


def _state_from_slabs(st):
    depth, _, n, _ = st.shape
    st = jnp.transpose(st, (0, 2, 1, 3))
    return (st[..., :SLAB_STATES].reshape(depth, n, SSM_GROUPS, SSM_STATE),
            st[..., SLAB_STATES:].reshape(depth, n, SSM_GROUPS, SSM_STATE))


def kernel(x_prompt, x_sample, mem_prompt, cache_conv, state_ssm_re, state_ssm_im, cache_mem_k, cache_mem_v, norm_g, w_in, conv_w, conv_b, ssm_lambda_re, ssm_lambda_im, ssm_log_dt, ssm_b_re, ssm_b_im, ssm_c_re, ssm_c_im, ssm_d, ssm_glu_w, ssm_glu_b, mem_norm_g, w_kv, w_out, final_norm_g):
    depth = w_in.shape[0]
    bp, seq, _ = x_prompt.shape
    bs, seq_s, _ = x_sample.shape
    assert bp == 1 and bs == N_SEG
    ls_p = 64
    assert seq % (N_SEG * ls_p) == 0 and seq_s % 8 == 0
    xp = x_prompt.reshape(seq // (N_SEG * ls_p), N_SEG, ls_p, D)
    xs = x_sample.reshape(1, N_SEG, seq_s, D)
    w_in_b, glu_b16, w_out_b = w_in.astype(BF16), ssm_glu_w.astype(BF16), w_out.astype(BF16)
    bblk, cblk, tab = _discretize(ssm_lambda_re, ssm_lambda_im, ssm_log_dt,
                                  ssm_b_re, ssm_b_im, ssm_c_re, ssm_c_im, ls_p)
    kp, vp, kp_b, vp_b = _mem_kv(mem_prompt[0], mem_norm_g, w_kv)
    ks_b, vs_b = _cache_kv(cache_mem_k, cache_mem_v)
    h0 = jnp.concatenate([jnp.transpose(_slab_cols(state_ssm_re), (0, 2, 1, 3)),
                          jnp.transpose(_slab_cols(state_ssm_im), (0, 2, 1, 3))], axis=-1)
    cinit = jnp.transpose(cache_conv, (0, 2, 1, 3))
    row = lambda a: a.reshape(depth, 1, D)
    small = [row(norm_g), conv_w, row(conv_b), tab, row(ssm_d), row(ssm_glu_b),
             final_norm_g.reshape(1, D)]
    weights = [w_in_b, glu_b16, w_out_b, bblk, cblk]
    p_cv, p_st, s_cv, s_st = ([] for _ in range(4))
    for l in range(depth):
        last = l == depth - 1
        xp, cv, st = _layer(xp, weights + [kp_b, vp_b], small, layer=l, chained=True,
                            final_norm=last, name=f"layer{l}_prompt")
        p_cv.append(cv)
        p_st.append(st)
        xs, cv, st = _layer(xs, weights + [ks_b, vs_b], small + [cinit, h0], layer=l, chained=False,
                            final_norm=last, name=f"layer{l}_sample")
        s_cv.append(cv)
        s_st.append(st)
    p_cv, p_st, s_cv, s_st = (jnp.stack(a) for a in (p_cv, p_st, s_cv, s_st))
    p_re, p_im = _state_from_slabs(p_st[:, :, N_SEG - 1:, :])
    s_re, s_im = _state_from_slabs(s_st)
    kv_shape = (depth, 1, N_MEM, HEADS, HEAD_DIM)
    return (xp.reshape(bp, seq, D), xs.reshape(bs, seq_s, D),
            p_cv[:, :, N_SEG - 1, :][:, None], p_re, p_im,
            kp.reshape(kv_shape), vp.reshape(kv_shape),
            jnp.transpose(s_cv, (0, 2, 1, 3)), s_re, s_im)
```

```python
import functools

import jax
import jax.numpy as jnp
from jax import lax
from jax.experimental import pallas as pl
from jax.experimental.pallas import tpu as pltpu

F32 = jnp.float32
BF16 = jnp.bfloat16

D = 1024
N_SEG = 8
SSM_GROUP = 16
SSM_STATE = 64
SSM_GROUPS = 64
N_SLAB = 4
SLAB = 256
SLAB_GROUPS = 16
SLAB_STATES = 1024
HEADS = 4
HEAD_DIM = 256
N_MEM = 256
CONV_CHUNK = 512
EPS = 1e-6
MASK_VALUE = -1e30
N_RESIDENT = 7
W_PIECE = 512
VMEM_LIMIT = 60000 * 1024


def _dot(a, b):
    return jnp.dot(a, b, preferred_element_type=F32)


def _rms(x, g):
    ms = jnp.mean(x * x, axis=-1, keepdims=True)
    return x * lax.rsqrt(ms + EPS) * g


def _disc_kernel(lre_ref, lim_ref, ldt_ref, bre_ref, bim_ref, cre_ref, cim_ref,
                 pre_ref, pim_ref, bblk_ref, cblk_ref, *, ls):
    i32 = jnp.int32
    spread_b = (lax.broadcasted_iota(i32, (SSM_STATE, SLAB_STATES), 1) % SSM_STATE
                == lax.broadcasted_iota(i32, (SSM_STATE, SLAB_STATES), 0)).astype(BF16)
    own_b = (lax.broadcasted_iota(i32, (SLAB, SLAB_STATES), 0) // SSM_GROUP
             == lax.broadcasted_iota(i32, (SLAB, SLAB_STATES), 1) // SSM_STATE)
    spread_c = (lax.broadcasted_iota(i32, (SSM_GROUP, SLAB), 1) % SSM_GROUP
                == lax.broadcasted_iota(i32, (SSM_GROUP, SLAB), 0)).astype(BF16)
    own_c = (lax.broadcasted_iota(i32, (SLAB_STATES, SLAB), 0) // SSM_STATE
             == lax.broadcasted_iota(i32, (SLAB_STATES, SLAB), 1) // SSM_GROUP)

    def block_diag(parts, spread, own):
        rows = jnp.concatenate(parts, axis=0).astype(BF16)
        return jnp.where(own, _dot(rows, spread), 0.0)

    for l in range(lre_ref.shape[0]):
        lr = lre_ref[l]
        li = lim_ref[l]
        dt = jnp.exp(ldt_ref[l])
        ea = jnp.exp(lr * dt)
        ar = ea * jnp.cos(li * dt)
        ai = ea * jnp.sin(li * dt)
        nr = ar - 1.0
        den = lr * lr + li * li
        fr = (nr * lr + ai * li) / den
        fi = (ai * lr - nr * li) / den
        for s in range(N_SLAB):
            groups = range(s * SLAB_GROUPS, (s + 1) * SLAB_GROUPS)
            bb_re = [fr[g:g + 1, :] * bre_ref[l, g] - fi[g:g + 1, :] * bim_ref[l, g] for g in groups]
            bb_im = [fr[g:g + 1, :] * bim_ref[l, g] + fi[g:g + 1, :] * bre_ref[l, g] for g in groups]
            bblk_ref[l, s, :, 0:SLAB_STATES] = block_diag(bb_re, spread_b, own_b).astype(BF16)
            bblk_ref[l, s, :, SLAB_STATES:] = block_diag(bb_im, spread_b, own_b).astype(BF16)
            cblk_ref[l, s, 0:SLAB_STATES, :] = block_diag(
                [cre_ref[l, g] for g in groups], spread_c, own_c).astype(BF16)
            cblk_ref[l, s, SLAB_STATES:, :] = (-block_diag(
                [cim_ref[l, g] for g in groups], spread_c, own_c)).astype(BF16)
        pre_ref[l, 0] = ar
        pim_ref[l, 0] = ai
        sr, si = ar, ai
        for _ in range(ls - 1):
            sr, si = sr * ar - si * ai, sr * ai + si * ar
        cr, ci = sr, si
        for m in range(1, N_SEG + 1):
            pre_ref[l, m] = cr
            pim_ref[l, m] = ci
            cr, ci = cr * sr - ci * si, cr * si + ci * sr


def _slab_cols(a):
    return a.reshape(a.shape[:-2] + (N_SLAB, SLAB_STATES))


def _discretize(lam_re, lam_im, log_dt, b_re, b_im, c_re, c_im, ls):
    depth, g, n = lam_re.shape
    swap = lambda a: jnp.transpose(a, (0, 1, 3, 2))
    npow = 1 + N_SEG
    p_re, p_im, bblk, cblk = pl.pallas_call(
        functools.partial(_disc_kernel, ls=ls),
        out_shape=(jax.ShapeDtypeStruct((depth, npow, g, n), F32),
                   jax.ShapeDtypeStruct((depth, npow, g, n), F32),
                   jax.ShapeDtypeStruct((depth, N_SLAB, SLAB, 2 * SLAB_STATES), BF16),
                   jax.ShapeDtypeStruct((depth, N_SLAB, 2 * SLAB_STATES, SLAB), BF16)),
        compiler_params=pltpu.CompilerParams(vmem_limit_bytes=VMEM_LIMIT),
        name="ssm_discretize",
    )(lam_re, lam_im, log_dt.reshape(depth, g, 1), swap(b_re), swap(b_im), swap(c_re), swap(c_im))
    tab = jnp.transpose(jnp.concatenate([_slab_cols(p_re), _slab_cols(p_im)], axis=-1), (0, 2, 1, 3))
    return bblk, cblk, tab


def _kv_kernel(mem_ref, g_ref, w_ref, k_ref, v_ref, kb_ref, vb_ref):
    x = mem_ref[...]
    for l in range(g_ref.shape[0]):
        h = _rms(x, g_ref[l:l + 1, :]).astype(BF16)
        kv = _dot(h, w_ref[l].astype(BF16))
        k_ref[l] = kv[:, :D]
        v_ref[l] = kv[:, D:]
        kb_ref[l] = kv[:, :D].astype(BF16)
        vb_ref[l] = kv[:, D:].astype(BF16)


def _mem_kv(mem, g, w_kv):
    depth = g.shape[0]
    n = mem.shape[0]
    f32 = jax.ShapeDtypeStruct((depth, n, D), F32)
    b16 = jax.ShapeDtypeStruct((depth, n, D), BF16)
    return pl.pallas_call(
        _kv_kernel,
        out_shape=(f32, f32, b16, b16),
        compiler_params=pltpu.CompilerParams(vmem_limit_bytes=VMEM_LIMIT),
        name="mem_kv",
    )(mem, g, w_kv)


def _cache_copies(k_hbm, v_hbm, stage, sems, n, slot, batch):
    return [pltpu.make_async_copy(src.at[n // batch, n % batch, :, hd, :], stage.at[slot, i, hd],
                                  sems.at[slot, i, hd])
            for i, src in enumerate((k_hbm, v_hbm)) for hd in range(HEADS)]


def _cache_kernel(k_hbm, v_hbm, ko_ref, vo_ref, stage, sems, *, batch):
    n = pl.program_id(0)
    slot = n % 2

    @pl.when(n == 0)
    def _():
        for cp in _cache_copies(k_hbm, v_hbm, stage, sems, 0, 0, batch):
            cp.start()

    @pl.when(n + 1 < pl.num_programs(0))
    def _():
        for cp in _cache_copies(k_hbm, v_hbm, stage, sems, n + 1, 1 - slot, batch):
            cp.start()

    for cp in _cache_copies(k_hbm, v_hbm, stage, sems, n, slot, batch):
        cp.wait()
    for i, out in enumerate((ko_ref, vo_ref)):
        for hd in range(HEADS):
            out[0, :, hd * HEAD_DIM:(hd + 1) * HEAD_DIM] = stage[slot, i, hd].astype(BF16)


def _cache_kv(cache_k, cache_v):
    depth, batch, n_mem, heads, head_dim = cache_k.shape
    assert (n_mem, heads, head_dim) == (N_MEM, HEADS, HEAD_DIM)
    out = jax.ShapeDtypeStruct((depth, batch * n_mem, D), BF16)
    blk = pl.BlockSpec((1, n_mem, D), lambda n: (n // batch, n % batch, 0))
    return pl.pallas_call(
        functools.partial(_cache_kernel, batch=batch),
        grid=(depth * batch,),
        in_specs=[pl.BlockSpec(memory_space=pl.ANY), pl.BlockSpec(memory_space=pl.ANY)],
        out_specs=(blk, blk),
        out_shape=(out, out),
        scratch_shapes=[pltpu.VMEM((2, 2, HEADS, N_MEM, HEAD_DIM), F32),
                        pltpu.SemaphoreType.DMA((2, 2, HEADS))],
        compiler_params=pltpu.CompilerParams(dimension_semantics=("arbitrary",)),
        name="cache_kv",
    )(cache_k, cache_v)


def _cmul(ar, ai, br, bi):
    return ar * br - ai * bi, ar * bi + ai * br


def _tile_copies(hbm, buf, sems, t, slot, to_hbm):
    cps = []
    for j in range(N_SEG):
        vm = buf.at[slot, :, j, :]
        hb = hbm.at[t, j]
        src, dst = (vm, hb) if to_hbm else (hb, vm)
        cps.append(pltpu.make_async_copy(src, dst, sems.at[slot, j]))
    return cps


def _layer_kernel(*refs, ls, layer, chained, final_norm):
    rows = N_SEG * ls
    it = iter(refs)
    x_hbm, win_hbm, gw_hbm, wout_hbm, bblk_hbm, cblk_hbm, k_hbm, v_hbm = (next(it) for _ in range(8))
    ng_ref, cw_ref, cb_ref, tab_ref, dssm_ref, gb_ref, fg_ref = (next(it) for _ in range(7))
    if not chained:
        cinit_ref, h0_ref = next(it), next(it)
    o_hbm, convout_ref, ssmout_ref = (next(it) for _ in range(3))
    if chained:
        winb_hbm, gwb_hbm, woutb_hbm = (next(it) for _ in range(3))
    xbuf, obuf, sem_in, sem_out = (next(it) for _ in range(4))
    win_ref, gw_ref, wout_ref, bblk_ref, cblk_ref, k_ref, v_ref, sem_w = (next(it) for _ in range(8))
    if chained:
        wstage, sem_stage, sem_export = (next(it) for _ in range(3))
    hbf, vext, bus, hbs, su32, ys32, ysb, mix, sgs, oacc, qs, qgs = (next(it) for _ in range(12))
    if chained:
        tail, carry = next(it), next(it)

    t = pl.program_id(0)
    nt = pl.num_programs(0)
    slot = t % 2
    resident = [(bblk_hbm.at[layer], bblk_ref), (cblk_hbm.at[layer], cblk_ref),
                (k_hbm.at[layer], k_ref), (v_hbm.at[layer], v_ref)]
    if chained:
        n_chunk = D // CONV_CHUNK
        pieces = []
        for kind in range(4):
            for c in range(n_chunk):
                src_lo = kind * D + c * CONV_CHUNK
                dst_lo = (c * 4 + kind) * CONV_CHUNK
                pieces.append((win_hbm.at[layer, :, src_lo:src_lo + CONV_CHUNK],
                               win_ref.at[:, dst_lo:dst_lo + CONV_CHUNK]))
        for lo in range(4 * D, 8 * D, W_PIECE):
            pieces.append((win_hbm.at[layer, :, lo:lo + W_PIECE], win_ref.at[:, lo:lo + W_PIECE]))
        for lo in range(0, D, W_PIECE):
            pieces.append((gw_hbm.at[layer, :, lo:lo + W_PIECE], gw_ref.at[:, lo:lo + W_PIECE]))
            for src_blk, dst_blk in ((0, 0), (2, 1), (1, 2)):
                pieces.append((wout_hbm.at[layer, src_blk * D:(src_blk + 1) * D, lo:lo + W_PIECE],
                               wout_ref.at[dst_blk * D:(dst_blk + 1) * D, lo:lo + W_PIECE]))

        def stage_copy(n):
            return pltpu.make_async_copy(pieces[n][0], wstage.at[n % 2], sem_stage.at[n % 2])

        exports = [pltpu.make_async_copy(src, dst, sem_export.at[n]) for n, (src, dst) in enumerate(
            ((win_ref, winb_hbm), (gw_ref, gwb_hbm), (wout_ref, woutb_hbm)))]
    else:
        resident += [(win_hbm, win_ref), (gw_hbm, gw_ref), (wout_hbm, wout_ref)]
    resident = [pltpu.make_async_copy(src, dst, sem_w.at[n]) for n, (src, dst) in enumerate(resident)]

    ng_ref, cw_ref, cb_ref, tab_ref, dssm_ref, gb_ref = (
        r.at[layer] for r in (ng_ref, cw_ref, cb_ref, tab_ref, dssm_ref, gb_ref))
    if not chained:
        cinit_ref, h0_ref = cinit_ref.at[layer], h0_ref.at[layer]

    @pl.when(t == 0)
    def _():
        for cp in resident:
            cp.start()
        for cp in _tile_copies(x_hbm, xbuf, sem_in, 0, 0, False):
            cp.start()
        if chained:
            tail[...] = jnp.zeros_like(tail)
            carry[...] = jnp.zeros_like(carry)

    @pl.when(t + 1 < nt)
    def _():
        for cp in _tile_copies(x_hbm, xbuf, sem_in, t + 1, 1 - slot, False):
            cp.start()

    for cp in _tile_copies(x_hbm, xbuf, sem_in, t, slot, False):
        cp.wait()

    @pl.when(t == 0)
    def _():
        if chained:
            stage_copy(0).start()
            for n, (_, dst) in enumerate(pieces):
                if n + 1 < len(pieces):
                    stage_copy(n + 1).start()
                stage_copy(n).wait()
                dst[...] = wstage[n % 2].astype(BF16)
            for cp in exports:
                cp.start()
        for cp in resident:
            cp.wait()

    xv = xbuf[slot].reshape(rows, D)
    hbf[...] = _rms(xv, ng_ref[...]).astype(BF16)

    re = slice(0, SLAB_STATES)
    im = slice(SLAB_STATES, 2 * SLAB_STATES)
    blk = (N_SEG, SLAB_STATES)
    sub_c = lax.broadcasted_iota(jnp.int32, (N_SEG, CONV_CHUNK), 0)
    sub_s = lax.broadcasted_iota(jnp.int32, blk, 0)

    def conv_chunk(c):
        lo = c * CONV_CHUNK
        sl = slice(lo, lo + CONV_CHUNK)
        z = _dot(hbf[...], win_ref[:, 4 * lo:4 * (lo + CONV_CHUNK)])
        cx, cb, cc, cg = (z[:, n * CONV_CHUNK:(n + 1) * CONV_CHUNK] for n in range(4))
        v = cc * cx
        last2 = v[rows - 16:rows - 8]
        last1 = v[rows - 8:rows]
        if chained:
            b2 = pltpu.roll(jnp.where(sub_c == N_SEG - 1, tail[0:8, sl], last2), 1, 0)
            b1 = pltpu.roll(jnp.where(sub_c == N_SEG - 1, tail[8:16, sl], last1), 1, 0)
            tail[0:8, sl] = last2
            tail[8:16, sl] = last1
        else:
            b2 = cinit_ref[0, :, sl]
            b1 = cinit_ref[1, :, sl]
        convout_ref[0, :, sl] = last2
        convout_ref[1, :, sl] = last1
        vx = vext.at[c]
        vx[0:8] = b2
        vx[8:16] = b1
        vx[16:16 + rows] = v
        conv = (cw_ref[0:1, sl] * vx[0:rows] + cw_ref[1:2, sl] * vx[8:8 + rows]
                + cw_ref[2:3, sl] * vx[16:16 + rows] + cb_ref[:, sl])
        mix[:, sl] = (cb * conv * jax.nn.silu(cg)).astype(BF16)

    n_keys = k_ref.shape[0]
    if not chained:
        row_seq = lax.broadcasted_iota(jnp.int32, (rows, n_keys), 0) % N_SEG
        key_seq = lax.broadcasted_iota(jnp.int32, (rows, n_keys), 1) // N_MEM
        own = row_seq == key_seq
    scale = HEAD_DIM ** -0.5

    def attn_head(hd):
        hs = slice(hd * HEAD_DIM, (hd + 1) * HEAD_DIM)
        sc = lax.dot_general(qs[:, hs].astype(BF16), k_ref[:, hs], (((1,), (1,)), ((), ())),
                             preferred_element_type=F32) * scale
        if not chained:
            sc = jnp.where(own, sc, MASK_VALUE)
        e = jnp.exp(sc - jnp.max(sc, axis=-1, keepdims=True))
        p = e / jnp.sum(e, axis=-1, keepdims=True)
        o = _dot(p.astype(BF16), v_ref[:, hs])
        mix[:, D + hd * HEAD_DIM:D + (hd + 1) * HEAD_DIM] = (o * jax.nn.silu(qgs[:, hs])).astype(BF16)

    def attn_proj():
        z = _dot(hbf[...], win_ref[:, 6 * D:8 * D])
        qs[...] = z[:, :D]
        qgs[...] = z[:, D:]

    def out_proj_conv_attn():
        oacc[...] = _dot(mix[:, 0:2 * D], wout_ref[0:2 * D, :])

    between = {0: [lambda: conv_chunk(0)], 1: [lambda: conv_chunk(1)],
               2: [attn_proj, lambda: attn_head(0), lambda: attn_head(1)],
               3: [lambda: attn_head(2), lambda: attn_head(3), out_proj_conv_attn]}

    z = _dot(hbf[...], win_ref[:, 4 * D:6 * D])
    su32[...] = z[:, :D]
    sgs[...] = z[:, D:]
    for s in range(N_SLAB):
        cs = slice(s * SLAB, (s + 1) * SLAB)
        bu, hb = bus.at[s % 2], hbs.at[s % 2]
        bu[...] = _dot(su32[:, cs].astype(BF16), bblk_ref[s])
        for work in between[s]:
            work()
        lr = jnp.broadcast_to(tab_ref[s, 0:1, re], blk)
        li = jnp.broadcast_to(tab_ref[s, 0:1, im], blk)

        def recurrence(hr, hi, emit):
            for i2 in range(ls // 2):
                blocks = []
                for i in (2 * i2, 2 * i2 + 1):
                    rw = slice(8 * i, 8 * i + 8)
                    hr, hi = lr * hr - li * hi + bu[rw, re], lr * hi + li * hr + bu[rw, im]
                    blocks.append(jnp.concatenate([hr, hi], axis=1))
                if emit:
                    hb[16 * i2:16 * i2 + 16, :] = jnp.concatenate(blocks, axis=0).astype(BF16)
            return hr, hi

        if chained:
            zero = jnp.zeros(blk, F32)
            tr, ti = recurrence(zero, zero, False)
            c0r = carry[s, 7:8, re]
            c0i = carry[s, 7:8, im]
            for sh in (1, 2, 4):
                keep = sub_s >= sh
                qr = jnp.where(keep, tab_ref[s, sh:sh + 1, re], 0.0)
                qi = jnp.where(keep, tab_ref[s, sh:sh + 1, im], 0.0)
                dr, di = _cmul(qr, qi, pltpu.roll(tr, sh, 0), pltpu.roll(ti, sh, 0))
                tr, ti = tr + dr, ti + di
            dr, di = _cmul(tab_ref[s, 1:1 + N_SEG, re], tab_ref[s, 1:1 + N_SEG, im], c0r, c0i)
            tr, ti = tr + dr, ti + di
            carry[s, :, re] = tr
            carry[s, :, im] = ti
            ssmout_ref[s, :, re] = tr
            ssmout_ref[s, :, im] = ti
            recurrence(jnp.where(sub_s == 0, c0r, pltpu.roll(tr, 1, 0)),
                       jnp.where(sub_s == 0, c0i, pltpu.roll(ti, 1, 0)), True)
        else:
            er, ei = recurrence(h0_ref[s, :, re], h0_ref[s, :, im], True)
            ssmout_ref[s, :, re] = er
            ssmout_ref[s, :, im] = ei

        y = _dot(hb[...], cblk_ref[s]) + dssm_ref[:, cs] * su32[:, cs]
        ys = jax.nn.gelu(y)
        ys32[:, cs] = ys
        ysb[:, cs] = ys.astype(BF16)

    gate = _dot(ysb[...], gw_ref[...]) + gb_ref[...]
    mix[:, 2 * D:3 * D] = (ys32[...] * jax.nn.sigmoid(gate) * jax.nn.silu(sgs[...])).astype(BF16)

    y = xbuf[slot].reshape(rows, D) + (oacc[...] + _dot(mix[:, 2 * D:3 * D], wout_ref[2 * D:3 * D, :]))
    if final_norm:
        y = _rms(y, fg_ref[...])

    @pl.when(t >= 2)
    def _():
        for cp in _tile_copies(o_hbm, obuf, sem_out, t - 2, slot, True):
            cp.wait()

    obuf[slot] = y.reshape(ls, N_SEG, D)
    for cp in _tile_copies(o_hbm, obuf, sem_out, t, slot, True):
        cp.start()

    @pl.when(t == nt - 1)
    def _():
        @pl.when(t >= 1)
        def _():
            for cp in _tile_copies(o_hbm, obuf, sem_out, t - 1, 1 - slot, True):
                cp.wait()
        for cp in _tile_copies(o_hbm, obuf, sem_out, t, slot, True):
            cp.wait()
        if chained:
            for cp in exports:
                cp.wait()


def _vmem_spec():
    return pl.BlockSpec(memory_space=pltpu.VMEM)


def _layer(x4, big, small, *, layer, chained, final_norm, name):
    nt, _, ls, _ = x4.shape
    rows = N_SEG * ls
    in_hbm = [x4] + list(big)
    w_shapes = [(D, 8 * D), (D, D), (3 * D, D)]
    any_spec = pl.BlockSpec(memory_space=pl.ANY)
    out_specs = [any_spec, pl.BlockSpec((2, N_SEG, D), lambda t: (0, 0, 0)),
                 pl.BlockSpec((N_SLAB, N_SEG, 2 * SLAB_STATES), lambda t: (0, 0, 0))]
    out_shape = [jax.ShapeDtypeStruct(x4.shape, F32), jax.ShapeDtypeStruct((2, N_SEG, D), F32),
                 jax.ShapeDtypeStruct((N_SLAB, N_SEG, 2 * SLAB_STATES), F32)]
    if chained:
        out_specs += [any_spec] * 3
        out_shape += [jax.ShapeDtypeStruct(shape, BF16) for shape in w_shapes]
    in_vmem = list(small)
    args = in_hbm + in_vmem
    in_specs = [pl.BlockSpec(memory_space=pl.ANY) for _ in in_hbm] + [_vmem_spec() for _ in in_vmem]
    scratch = [
        pltpu.VMEM((min(nt, 2), ls, N_SEG, D), F32),
        pltpu.VMEM((min(nt, 2), ls, N_SEG, D), F32),
        pltpu.SemaphoreType.DMA((2, N_SEG)),
        pltpu.SemaphoreType.DMA((2, N_SEG)),
    ] + [pltpu.VMEM(shape, BF16) for shape in w_shapes] + [
        pltpu.VMEM(a.shape[1:], a.dtype) for a in big[3:]] + [
        pltpu.SemaphoreType.DMA((N_RESIDENT,)),
    ] + ([pltpu.VMEM((2, D, W_PIECE), F32), pltpu.SemaphoreType.DMA((2,)),
          pltpu.SemaphoreType.DMA((3,))] if chained else []) + [
        pltpu.VMEM((rows, D), BF16),
        pltpu.VMEM((D // CONV_CHUNK, rows + 16, CONV_CHUNK), F32),
        pltpu.VMEM((2, rows, 2 * SLAB_STATES), F32),
        pltpu.VMEM((2, rows, 2 * SLAB_STATES), BF16),
        pltpu.VMEM((rows, D), F32),
        pltpu.VMEM((rows, D), F32),
        pltpu.VMEM((rows, D), BF16),
        pltpu.VMEM((rows, 3 * D), BF16),
        pltpu.VMEM((rows, D), F32),
        pltpu.VMEM((rows, D), F32),
        pltpu.VMEM((rows, D), F32),
        pltpu.VMEM((rows, D), F32),
    ]
    if chained:
        scratch += [pltpu.VMEM((16, D), F32), pltpu.VMEM((N_SLAB, N_SEG, 2 * SLAB_STATES), F32)]
    return pl.pallas_call(
        functools.partial(_layer_kernel, ls=ls, layer=layer, chained=chained, final_norm=final_norm),
        grid=(nt,),
        in_specs=in_specs,
        out_specs=tuple(out_specs),
        out_shape=tuple(out_shape),
        scratch_shapes=scratch,
        compiler_params=pltpu.CompilerParams(dimension_semantics=("arbitrary",),
                                             vmem_limit_bytes=VMEM_LIMIT),
        name=name,
    )(*args)


def _state_from_slabs(st):
    depth, _, n, _ = st.shape
    st = jnp.transpose(st, (0, 2, 1, 3))
    return (st[..., :SLAB_STATES].reshape(depth, n, SSM_GROUPS, SSM_STATE),
            st[..., SLAB_STATES:].reshape(depth, n, SSM_GROUPS, SSM_STATE))


def kernel(x_prompt, x_sample, mem_prompt, cache_conv, state_ssm_re, state_ssm_im, cache_mem_k, cache_mem_v, norm_g, w_in, conv_w, conv_b, ssm_lambda_re, ssm_lambda_im, ssm_log_dt, ssm_b_re, ssm_b_im, ssm_c_re, ssm_c_im, ssm_d, ssm_glu_w, ssm_glu_b, mem_norm_g, w_kv, w_out, final_norm_g):
    depth = w_in.shape[0]
    bp, seq, _ = x_prompt.shape
    bs, seq_s, _ = x_sample.shape
    assert bp == 1 and bs == N_SEG
    ls_p = 32
    assert seq % (N_SEG * ls_p) == 0 and seq_s % 8 == 0
    xp = x_prompt.reshape(seq // (N_SEG * ls_p), N_SEG, ls_p, D)
    xs = x_sample.reshape(1, N_SEG, seq_s, D)
    bblk, cblk, tab = _discretize(ssm_lambda_re, ssm_lambda_im, ssm_log_dt,
                                  ssm_b_re, ssm_b_im, ssm_c_re, ssm_c_im, ls_p)
    kp, vp, kp_b, vp_b = _mem_kv(mem_prompt[0], mem_norm_g, w_kv)
    ks_b, vs_b = _cache_kv(cache_mem_k, cache_mem_v)
    h0 = jnp.concatenate([jnp.transpose(_slab_cols(state_ssm_re), (0, 2, 1, 3)),
                          jnp.transpose(_slab_cols(state_ssm_im), (0, 2, 1, 3))], axis=-1)
    cinit = jnp.transpose(cache_conv, (0, 2, 1, 3))
    row = lambda a: a.reshape(depth, 1, D)
    small = [row(norm_g), conv_w, row(conv_b), tab, row(ssm_d), row(ssm_glu_b),
             final_norm_g.reshape(1, D)]
    p_cv, p_st, s_cv, s_st = ([] for _ in range(4))
    for l in range(depth):
        last = l == depth - 1
        xp, cv, st, *w_b = _layer(xp, [w_in, ssm_glu_w, w_out, bblk, cblk, kp_b, vp_b], small, layer=l,
                                  chained=True, final_norm=last, name=f"layer{l}_prompt")
        p_cv.append(cv)
        p_st.append(st)
        xs, cv, st = _layer(xs, w_b + [bblk, cblk, ks_b, vs_b], small + [cinit, h0], layer=l,
                            chained=False, final_norm=last, name=f"layer{l}_sample")
        s_cv.append(cv)
        s_st.append(st)
    p_cv, p_st, s_cv, s_st = (jnp.stack(a) for a in (p_cv, p_st, s_cv, s_st))
    p_re, p_im = _state_from_slabs(p_st[:, :, N_SEG - 1:, :])
    s_re, s_im = _state_from_slabs(s_st)
    kv_shape = (depth, 1, N_MEM, HEADS, HEAD_DIM)
    return (xp.reshape(bp, seq, D), xs.reshape(bs, seq_s, D),
            p_cv[:, :, N_SEG - 1, :][:, None], p_re, p_im,
            kp.reshape(kv_shape), vp.reshape(kv_shape),
            jnp.transpose(s_cv, (0, 2, 1, 3)), s_re, s_im)
```
